```python
import math
import jax, jax.numpy as jnp
from jax import lax
import numpy as np


D_MODEL = 1024
BATCH = 8
SEQ = 8192
DEPTH = 1

PLE_DIM = 256
D_FF = 2816
RWKV_HEADS = 8
RWKV_HEAD_DIM = 64
RWKV_WIDTH = RWKV_HEADS * RWKV_HEAD_DIM
DECAY_LORA = 64
AAA_LORA = 64
GATE_LORA = 128
RWKV_GN_EPS = 64e-5
DECAY_SCALE = math.exp(-0.5)
RET_HEADS = 4
RET_QK_DIM = 128
RET_V_DIM = 256
RET_QK_WIDTH = RET_HEADS * RET_QK_DIM
RET_V_WIDTH = RET_HEADS * RET_V_DIM
RET_CHUNK = 128
ROPE_BASE = 10000.0
RWKV_IN = 3 * RWKV_WIDTH + DECAY_LORA + AAA_LORA + GATE_LORA
RET_IN = 2 * RET_QK_WIDTH + 2 * RET_V_WIDTH
GATE_IN = 2 * D_MODEL
MIX_IN = RWKV_IN + RET_IN + GATE_IN
DN_ALPHA = (2 * DEPTH) ** 0.25
DN_BETA = (8 * DEPTH) ** -0.25
LN_EPS = 1e-5

kernel_name = "hybrid_rwkv7_retention_macaron_deepnorm"


def layer_norm(x, g, b, eps=LN_EPS):
    xf = x.astype(jnp.float32)
    mu = jnp.mean(xf, -1, keepdims=True)
    var = jnp.mean(jnp.square(xf - mu), -1, keepdims=True)
    return ((xf - mu) * lax.rsqrt(var + eps) * g + b).astype(x.dtype)


def head_norm(y, eps):
    mu = jnp.mean(y, -1, keepdims=True)
    var = jnp.mean(jnp.square(y - mu), -1, keepdims=True)
    return (y - mu) * lax.rsqrt(var + eps)


def swiglu(x, w_in, w_out):
    gate, up = jnp.split(x @ w_in, 2, axis=-1)
    return (jax.nn.silu(gate) * up) @ w_out


def token_shift(z):
    return jnp.pad(z[:, :-1], ((0, 0), (1, 0), (0, 0)))


def rwkv_step(state, inp):
    r_t, w_t, k_t, v_t, kk_t, a_t = inp
    sa = jnp.einsum('bhvk,bhk->bhv', state, kk_t)
    state = (state * w_t[:, :, None, :]
             - sa[..., None] * (kk_t * a_t)[:, :, None, :]
             + v_t[..., None] * k_t[:, :, None, :])
    return state, jnp.einsum('bhvk,bhk->bhv', state, r_t)


def rwkv7_mix(z, mu, w0, w_up, a0, a_up, g_up, k_k, k_a, r_k, gn_g, gn_b):
    B, S, _ = z.shape
    H, N, W = RWKV_HEADS, RWKV_HEAD_DIM, RWKV_WIDTH
    f32 = jnp.float32
    z = z + (token_shift(z) - z) * mu
    r, k, v, dw, da, dg = jnp.split(
        z, [W, 2 * W, 3 * W, 3 * W + DECAY_LORA, 3 * W + DECAY_LORA + AAA_LORA], axis=-1)
    log_w = -DECAY_SCALE * jax.nn.sigmoid((w0 + jnp.tanh(dw) @ w_up).astype(f32))
    a = jax.nn.sigmoid((a0 + da @ a_up).astype(f32))
    g = (jax.nn.sigmoid(dg) @ g_up).astype(f32)
    kk = (k * k_k).astype(f32).reshape(B, S, H, N)
    kk = kk * lax.rsqrt(jnp.maximum(jnp.sum(jnp.square(kk), -1, keepdims=True), 1e-24))
    k = (k.astype(f32) * (1.0 + (a - 1.0) * k_a)).reshape(B, S, H, N)
    r = r.astype(f32).reshape(B, S, H, N)
    v = v.astype(f32).reshape(B, S, H, N)
    a = a.reshape(B, S, H, N)
    w = jnp.exp(log_w).reshape(B, S, H, N)
    tm = lambda t: jnp.moveaxis(t, 1, 0)
    state0 = jnp.zeros((B, H, N, N), f32)
    _, y = lax.scan(rwkv_step, state0, (tm(r), tm(w), tm(k), tm(v), tm(kk), tm(a)))
    y = jnp.moveaxis(y, 0, 1)
    y = head_norm(y, RWKV_GN_EPS).reshape(B, S, W) * gn_g + gn_b
    bonus = jnp.sum(r * k * r_k, -1, keepdims=True) * v
    y = y + bonus.reshape(B, S, W)
    return (y * g).astype(z.dtype)


def rotary(x, positions):
    half = x.shape[-1] // 2
    inv_freq = ROPE_BASE ** (-jnp.arange(half, dtype=jnp.float32) / half)
    ang = positions.astype(jnp.float32)[..., None] * inv_freq
    cos = jnp.cos(ang)[:, :, None, :]
    sin = jnp.sin(ang)[:, :, None, :]
    x1, x2 = x[..., :half], x[..., half:]
    return jnp.concatenate([x1 * cos - x2 * sin, x2 * cos + x1 * sin], -1)


def retention_mix(z, positions):
    B, S, _ = z.shape
    H, Dk, Dv, C = RET_HEADS, RET_QK_DIM, RET_V_DIM, RET_CHUNK
    nC = S // C
    f32 = jnp.float32
    q, k, v, g = jnp.split(z, [RET_QK_WIDTH, 2 * RET_QK_WIDTH, 2 * RET_QK_WIDTH + RET_V_WIDTH], axis=-1)
    q = rotary(q.astype(f32).reshape(B, S, H, Dk), positions)
    k = rotary(k.astype(f32).reshape(B, S, H, Dk), positions) * (Dk ** -0.5)
    v = v.astype(f32).reshape(B, S, H, Dv)
    log_gamma = jnp.log(1.0 - jnp.exp2(-5.0 - jnp.arange(H, dtype=f32)))
    idx = jnp.arange(C, dtype=f32)
    rel = idx[:, None] - idx[None, :]
    decay_mask = jnp.where(rel >= 0, jnp.exp(log_gamma[:, None, None] * jnp.maximum(rel, 0.0)), 0.0)
    xi = jnp.exp(log_gamma[:, None] * (idx + 1.0))
    zeta = jnp.exp(log_gamma[:, None] * (C - 1.0 - idx))
    chunk_decay = jnp.exp(log_gamma * C)

    def to_chunks(t):
        return t.reshape(B, nC, C, H, t.shape[-1]).transpose(1, 0, 3, 2, 4)

    def step(R, qkv):
        qc, kc, vc = qkv
        scores = jnp.einsum('bhnd,bhmd->bhnm', qc, kc) * decay_mask
        inner = jnp.einsum('bhnm,bhme->bhne', scores, vc)
        cross = jnp.einsum('bhnd,bhde->bhne', qc, R) * xi[None, :, :, None]
        R = R * chunk_decay[None, :, None, None] + jnp.einsum('bhmd,bhme->bhde', kc * zeta[None, :, :, None], vc)
        return R, inner + cross

    R0 = jnp.zeros((B, H, Dk, Dv), f32)
    _, y = lax.scan(step, R0, (to_chunks(q), to_chunks(k), to_chunks(v)))
    y = y.transpose(1, 0, 3, 2, 4).reshape(B, S, H, Dv)
    y = head_norm(y, LN_EPS).reshape(B, S, RET_V_WIDTH)
    return (jax.nn.silu(g.astype(f32)) * y).astype(z.dtype)


def hybrid_mix(h, positions, w_in, mu, w0, w_up, a0, a_up, g_up, k_k, k_a, r_k, gn_g, gn_b,
               w_branch_rwkv, w_branch_ret, w_out):
    z = h @ w_in
    z_rwkv, z_ret, z_gate = jnp.split(z, [RWKV_IN, RWKV_IN + RET_IN], axis=-1)
    y_rwkv = rwkv7_mix(z_rwkv, mu, w0, w_up, a0, a_up, g_up, k_k, k_a, r_k, gn_g, gn_b) @ w_branch_rwkv
    y_ret = retention_mix(z_ret, positions) @ w_branch_ret
    gate_rwkv, gate_ret = jnp.split(jax.nn.sigmoid(z_gate), 2, axis=-1)
    return (gate_rwkv * y_rwkv + gate_ret * y_ret) @ w_out


def setup_inputs(seed: int = 0) -> dict:
    key = jax.random.key(seed)
    ks = iter(jax.random.split(key, 40))
    f32 = jnp.float32
    nrm = lambda shape, s: jax.random.normal(next(ks), shape, f32) * s
    L, D, F = DEPTH, D_MODEL, D_FF
    x = nrm((BATCH, SEQ, D), 1.0)
    p = nrm((DEPTH, BATCH, SEQ, PLE_DIM), 1.0)
    start = jax.random.randint(next(ks), (BATCH, 1), 0, 4096, dtype=jnp.int32)
    positions = (start + jnp.arange(SEQ, dtype=jnp.int32)[None, :]).astype(jnp.int32)
    return {
        "x": x,
        "p": p,
        "positions": positions,
        "ln1_g": 1.0 + nrm((L, D), 0.1),
        "ln1_b": nrm((L, D), 0.01),
        "ffn1_w_in": nrm((L, D, 2 * F), D ** -0.5),
        "ffn1_w_out": nrm((L, F, D), F ** -0.5 * DN_BETA),
        "w_mix_in": nrm((L, D, MIX_IN), D ** -0.5),
        "rwkv_mu": jax.random.uniform(next(ks), (L, RWKV_IN), f32),
        "rwkv_w0": nrm((L, RWKV_WIDTH), 1.0),
        "rwkv_w_up": nrm((L, DECAY_LORA, RWKV_WIDTH), 0.5 * DECAY_LORA ** -0.5),
        "rwkv_a0": nrm((L, RWKV_WIDTH), 0.5),
        "rwkv_a_up": nrm((L, AAA_LORA, RWKV_WIDTH), 0.5 * AAA_LORA ** -0.5),
        "rwkv_g_up": nrm((L, GATE_LORA, RWKV_WIDTH), GATE_LORA ** -0.5),
        "rwkv_k_k": 0.85 + nrm((L, RWKV_WIDTH), 0.1),
        "rwkv_k_a": 1.0 + nrm((L, RWKV_WIDTH), 0.1),
        "rwkv_r_k": nrm((L, RWKV_HEADS, RWKV_HEAD_DIM), 0.1),
        "rwkv_gn_g": 1.0 + nrm((L, RWKV_WIDTH), 0.1),
        "rwkv_gn_b": nrm((L, RWKV_WIDTH), 0.01),
        "w_branch_rwkv": nrm((L, RWKV_WIDTH, D), RWKV_WIDTH ** -0.5),
        "w_branch_ret": nrm((L, RET_V_WIDTH, D), RET_V_WIDTH ** -0.5),
        "w_mix_out": nrm((L, D, D), D ** -0.5 * DN_BETA),
        "ln2_g": 1.0 + nrm((L, D), 0.1),
        "ln2_b": nrm((L, D), 0.01),
        "ffn2_w_in": nrm((L, D, 2 * F), D ** -0.5),
        "ffn2_w_out": nrm((L, F, D), F ** -0.5 * DN_BETA),
        "ln3_g": 1.0 + nrm((L, D), 0.1),
        "ln3_b": nrm((L, D), 0.01),
        "ple_w_proj": nrm((L, PLE_DIM, D), PLE_DIM ** -0.5),
        "ple_w_gate": nrm((L, D, D), D ** -0.5),
    }


def reference(x, p, positions, ln1_g, ln1_b, ffn1_w_in, ffn1_w_out, w_mix_in, rwkv_mu, rwkv_w0, rwkv_w_up,
              rwkv_a0, rwkv_a_up, rwkv_g_up, rwkv_k_k, rwkv_k_a, rwkv_r_k, rwkv_gn_g, rwkv_gn_b,
              w_branch_rwkv, w_branch_ret, w_mix_out, ln2_g, ln2_b, ffn2_w_in, ffn2_w_out, ln3_g, ln3_b,
              ple_w_proj, ple_w_gate):
    h = x
    for i in range(DEPTH):
        h = layer_norm(DN_ALPHA * h + 0.5 * swiglu(h, ffn1_w_in[i], ffn1_w_out[i]), ln1_g[i], ln1_b[i])
        mix = hybrid_mix(h, positions, w_mix_in[i], rwkv_mu[i], rwkv_w0[i], rwkv_w_up[i], rwkv_a0[i],
                         rwkv_a_up[i], rwkv_g_up[i], rwkv_k_k[i], rwkv_k_a[i], rwkv_r_k[i], rwkv_gn_g[i],
                         rwkv_gn_b[i], w_branch_rwkv[i], w_branch_ret[i], w_mix_out[i])
        h = layer_norm(DN_ALPHA * h + mix, ln2_g[i], ln2_b[i])
        h = layer_norm(DN_ALPHA * h + 0.5 * swiglu(h, ffn2_w_in[i], ffn2_w_out[i]), ln3_g[i], ln3_b[i])
        h = h + jax.nn.sigmoid(h @ ple_w_gate[i]) * (p[i] @ ple_w_proj[i])
    return h
```

```python
import functools
import math

import jax
import jax.numpy as jnp
from jax import lax
from jax.experimental import pallas as pl
from jax.experimental.pallas import tpu as pltpu

F32 = jnp.float32
BF16 = jnp.bfloat16

D_MODEL = 1024
PLE_DIM = 256
D_FF = 2816
RWKV_HEADS = 8
RWKV_HEAD_DIM = 64
RWKV_WIDTH = RWKV_HEADS * RWKV_HEAD_DIM
DECAY_LORA = 64
AAA_LORA = 64
GATE_LORA = 128
RWKV_GN_EPS = 64e-5
DECAY_SCALE = math.exp(-0.5)
RET_HEADS = 4
RET_QK_DIM = 128
RET_V_DIM = 256
RET_QK_WIDTH = RET_HEADS * RET_QK_DIM
RET_V_WIDTH = RET_HEADS * RET_V_DIM
RET_CHUNK = 128
ROPE_BASE = 10000.0
RWKV_IN = 3 * RWKV_WIDTH + DECAY_LORA + AAA_LORA + GATE_LORA
RET_IN = 2 * RET_QK_WIDTH + 2 * RET_V_WIDTH
GATE_IN = 2 * D_MODEL
DEPTH = 1
DN_ALPHA = (2 * DEPTH) ** 0.25
LN_EPS = 1e-5

LANES = 128
RWKV_CHUNK = 64
RWKV_PAIRS = RWKV_WIDTH // LANES
VMEM_LIMIT = 56 * 1024 * 1024

HIGHEST = lax.Precision.HIGHEST


def _dot(a, b):
    return jnp.dot(a.astype(BF16), b.astype(BF16), preferred_element_type=F32)


def _dot_nt(a, b):
    return lax.dot_general(a.astype(BF16), b.astype(BF16), (((1,), (1,)), ((), ())),
                           preferred_element_type=F32)


def _dot_tn(a, b):
    return lax.dot_general(a.astype(BF16), b.astype(BF16), (((0,), (0,)), ((), ())),
                           preferred_element_type=F32)


def _dot_f32(a, b):
    return jnp.dot(a, b, preferred_element_type=F32, precision=HIGHEST)


def _layer_norm(y, g, b):
    mu = jnp.mean(y, -1, keepdims=True)
    yc = y - mu
    var = jnp.mean(yc * yc, -1, keepdims=True)
    return yc * lax.rsqrt(var + LN_EPS) * g + b


def _const_spec(shape):
    return pl.BlockSpec(shape, lambda *_: (0,) * len(shape))


def _ffn_kernel(x_ref, win_ref, wout_ref, g_ref, b_ref, o_ref, acc_ref, *, fc):
    x = x_ref[...]
    xb = x.astype(BF16)
    for j in range(D_FF // fc):
        gate = jnp.dot(xb, win_ref[:, j * fc:(j + 1) * fc], preferred_element_type=F32)
        up = jnp.dot(xb, win_ref[:, D_FF + j * fc:D_FF + (j + 1) * fc], preferred_element_type=F32)
        act = (gate * jax.nn.sigmoid(gate) * up).astype(BF16)
        part = jnp.dot(act, wout_ref[j * fc:(j + 1) * fc, :], preferred_element_type=F32)
        if j == 0:
            acc_ref[...] = part
        else:
            acc_ref[...] += part
    y = DN_ALPHA * x + 0.5 * acc_ref[...]
    o_ref[...] = _layer_norm(y, g_ref[...], b_ref[...])


def _ffn(x, w_in, w_out, g, b, *, tm=512, fc=256):
    n = x.shape[0]
    return pl.pallas_call(
        functools.partial(_ffn_kernel, fc=fc),
        grid=(n // tm,),
        in_specs=[
            pl.BlockSpec((tm, D_MODEL), lambda i: (i, 0)),
            _const_spec((D_MODEL, 2 * D_FF)),
            _const_spec((D_FF, D_MODEL)),
            _const_spec((1, D_MODEL)),
            _const_spec((1, D_MODEL)),
        ],
        out_specs=pl.BlockSpec((tm, D_MODEL), lambda i: (i, 0)),
        out_shape=jax.ShapeDtypeStruct((n, D_MODEL), F32),
        scratch_shapes=[pltpu.VMEM((tm, D_MODEL), F32)],
        compiler_params=pltpu.CompilerParams(
            dimension_semantics=("parallel",), vmem_limit_bytes=VMEM_LIMIT),
        name="ffn",
    )(x, w_in, w_out, g, b)


def _mixin_kernel(h_ref, w_ref, zr_ref, zt_ref, zg_ref):
    hb = h_ref[...].astype(BF16)
    zr_ref[...] = jnp.dot(hb, w_ref[:, :RWKV_IN], preferred_element_type=F32)
    zt_ref[...] = jnp.dot(hb, w_ref[:, RWKV_IN:RWKV_IN + RET_IN], preferred_element_type=F32)
    zg_ref[...] = jax.nn.sigmoid(
        jnp.dot(hb, w_ref[:, RWKV_IN + RET_IN:], preferred_element_type=F32))


def _mix_in(h, w, *, tm=256):
    n = h.shape[0]
    return pl.pallas_call(
        _mixin_kernel,
        grid=(n // tm,),
        in_specs=[
            pl.BlockSpec((tm, D_MODEL), lambda i: (i, 0)),
            _const_spec((D_MODEL, RWKV_IN + RET_IN + GATE_IN)),
        ],
        out_specs=[
            pl.BlockSpec((tm, RWKV_IN), lambda i: (i, 0)),
            pl.BlockSpec((tm, RET_IN), lambda i: (i, 0)),
            pl.BlockSpec((tm, GATE_IN), lambda i: (i, 0)),
        ],
        out_shape=[
            jax.ShapeDtypeStruct((n, RWKV_IN), F32),
            jax.ShapeDtypeStruct((n, RET_IN), F32),
            jax.ShapeDtypeStruct((n, GATE_IN), F32),
        ],
        compiler_params=pltpu.CompilerParams(
            dimension_semantics=("parallel",), vmem_limit_bytes=VMEM_LIMIT),
        name="mix_in",
    )(h, w)


def _rwkv_kernel(z_ref, mu_ref, wwa_ref, w0_ref, a0_ref, gup_ref, kk_ref, ka_ref, rk_ref,
                 gng_ref, gnb_ref, o_ref,
                 carry_ref, t_ref, aq_ref, rq_ref, kd_ref, bd_ref, ke_ref, be_ref, v_ref,
                 gl_ref, y_ref, *, tb):
    L = RWKV_CHUNK
    W = RWKV_WIDTH

    @pl.when(pl.program_id(1) == 0)
    def _():
        carry_ref[...] = jnp.zeros_like(carry_ref)
        t_ref[...] = jnp.zeros_like(t_ref)

    z = z_ref[0]
    row = lax.broadcasted_iota(jnp.int32, z.shape, 0)
    prev = jnp.where(row == 0, carry_ref[...], pltpu.roll(z, 1, axis=0))
    carry_ref[...] = z[tb - 1:tb, :]
    zs = z + (prev - z) * mu_ref[...]

    r = zs[:, :W]
    k = zs[:, W:2 * W]
    v = zs[:, 2 * W:3 * W]
    dwa = zs[:, 3 * W:3 * W + LANES]
    dg = zs[:, 3 * W + LANES:]
    lane = lax.broadcasted_iota(jnp.int32, (tb, LANES), 1)
    left = lane < RWKV_HEAD_DIM
    dwa = jnp.where(left, jnp.tanh(dwa), dwa)
    wa = _dot(dwa, wwa_ref[...])
    lw = -DECAY_SCALE * jax.nn.sigmoid(w0_ref[...] + wa[:, :W])
    a = jax.nn.sigmoid(a0_ref[...] + wa[:, W:])
    g = _dot(jax.nn.sigmoid(dg), gup_ref[...])

    jr = lax.broadcasted_iota(jnp.int32, (LANES, LANES), 0)
    jc = lax.broadcasted_iota(jnp.int32, (LANES, LANES), 1)
    same_head = (jr < RWKV_HEAD_DIM) == (jc < RWKV_HEAD_DIM)
    ones_bd = jnp.where(same_head, 1.0, 0.0).astype(F32)

    def head_sum(x):
        return jnp.concatenate(
            [_dot_f32(x[:, p * LANES:(p + 1) * LANES], ones_bd) for p in range(RWKV_PAIRS)], axis=1)

    kk = k * kk_ref[...]
    kk = kk * lax.rsqrt(jnp.maximum(head_sum(kk * kk), 1e-24))
    k2 = k * (1.0 + (a - 1.0) * ka_ref[...])
    bonus = head_sum(r * k2 * rk_ref[...]) * v

    tr = lax.broadcasted_iota(jnp.int32, (tb, tb), 0)
    tc = lax.broadcasted_iota(jnp.int32, (tb, tb), 1)
    chunk_shift = L.bit_length() - 1
    same_chunk = (tr >> chunk_shift) == (tc >> chunk_shift)
    tril = jnp.where(same_chunk & (tc <= tr), 1.0, 0.0).astype(F32)
    ones_chunk = jnp.where(same_chunk, 1.0, 0.0).astype(F32)
    cs = _dot_f32(tril, lw)
    gl = _dot_f32(ones_chunk, lw)

    ka = kk * a
    inv = jnp.exp(-cs)
    end = jnp.exp(gl - cs)
    aq_ref[...] = kk * jnp.exp(cs - lw)
    rq_ref[...] = r * jnp.exp(cs)
    kd_ref[...] = k2 * inv
    bd_ref[...] = ka * inv
    ke_ref[...] = k2 * end
    be_ref[...] = ka * end
    v_ref[...] = v
    gl_ref[...] = jnp.exp(gl)

    trow = lax.broadcasted_iota(jnp.int32, (L, LANES), 0)
    scol = lax.broadcasted_iota(jnp.int32, (L, LANES), 1)
    m_left = scol < RWKV_HEAD_DIM
    scol = scol & (L - 1)
    strict = scol < trow
    incl = scol <= trow
    eye_pair = jnp.where(scol == trow, 1.0, 0.0).astype(F32)
    diag = jr == jc

    def bd2(x):
        return jnp.concatenate([jnp.where(m_left, x, 0.0), jnp.where(m_left, 0.0, x)], axis=0)

    def chunk_body(c, carry):
        rows = pl.ds(pl.multiple_of(c * L, L), L)
        for p in range(RWKV_PAIRS):
            lanes = slice(p * LANES, (p + 1) * LANES)
            aq = aq_ref[rows, lanes]
            rq = rq_ref[rows, lanes]
            kd = kd_ref[rows, lanes]
            bd = bd_ref[rows, lanes]
            ke = ke_ref[rows, lanes]
            be = be_ref[rows, lanes]
            vv = v_ref[rows, lanes]
            gam_l = gl_ref[pl.ds(pl.multiple_of(c * L, L), 1), lanes]

            m = _dot_nt(jnp.concatenate([aq, rq], axis=0),
                        jnp.concatenate([bd2(kd), bd2(bd)], axis=0))
            mak = jnp.where(strict, m[:L, :LANES], 0.0)
            mab = jnp.where(strict, m[:L, LANES:], 0.0)
            mrk = jnp.where(incl, m[L:, :LANES], 0.0)
            mrb = jnp.where(incl, m[L:, LANES:], 0.0)

            npow = mab
            pinv = eye_pair - mab
            for _ in range(5):
                npow = _dot(npow, bd2(npow))
                pinv = pinv + _dot(pinv, bd2(npow))

            makv = _dot(mak, bd2(vv))
            au = _dot(pinv, jnp.concatenate([bd2(aq), bd2(makv)], axis=1))
            ahat = au[:, :LANES]
            uhat = au[:, LANES:]
            qhat = rq - _dot(mrb, bd2(ahat))
            yhat = _dot(jnp.concatenate([mrk, -mrb], axis=1),
                        jnp.concatenate([bd2(vv), bd2(uhat)], axis=0))

            t0 = t_ref[p]
            y_ref[rows, lanes] = _dot(qhat, t0) + yhat

            bta = _dot_tn(be, au)
            ktv = _dot_tn(ke, vv)
            gmat = jnp.where(same_head, jnp.where(diag, gam_l, 0.0) - bta[:, :LANES], 0.0)
            hmat = jnp.where(same_head, ktv - bta[:, LANES:], 0.0)
            t_ref[p] = _dot(gmat, t0) + hmat
        return carry

    lax.fori_loop(0, tb // L, chunk_body, 0)

    y = y_ref[...]
    mean = head_sum(y) * (1.0 / RWKV_HEAD_DIM)
    yc = y - mean
    var = head_sum(yc * yc) * (1.0 / RWKV_HEAD_DIM)
    yn = yc * lax.rsqrt(var + RWKV_GN_EPS) * gng_ref[...] + gnb_ref[...]
    o_ref[0] = (yn + bonus) * g


def _rwkv(z, mu, wwa, w0, a0, g_up, k_k, k_a, r_k, gn_g, gn_b, *, tb=256):
    b, s, _ = z.shape
    W = RWKV_WIDTH
    wide = pltpu.VMEM((tb, W), F32)
    return pl.pallas_call(
        functools.partial(_rwkv_kernel, tb=tb),
        grid=(b, s // tb),
        in_specs=[
            pl.BlockSpec((1, tb, RWKV_IN), lambda i, j: (i, j, 0)),
            _const_spec((1, RWKV_IN)),
            _const_spec((LANES, 2 * W)),
            _const_spec((1, W)),
            _const_spec((1, W)),
            _const_spec((GATE_LORA, W)),
            _const_spec((1, W)),
            _const_spec((1, W)),
            _const_spec((1, W)),
            _const_spec((1, W)),
            _const_spec((1, W)),
        ],
        out_specs=pl.BlockSpec((1, tb, W), lambda i, j: (i, j, 0)),
        out_shape=jax.ShapeDtypeStruct((b, s, W), F32),
        scratch_shapes=[
            pltpu.VMEM((1, RWKV_IN), F32),
            pltpu.VMEM((RWKV_PAIRS, LANES, LANES), F32),
            wide, wide, wide, wide, wide, wide, wide, wide, wide,
        ],
        compiler_params=pltpu.CompilerParams(
            dimension_semantics=("arbitrary", "arbitrary"), vmem_limit_bytes=VMEM_LIMIT),
        name="rwkv",
    )(z, mu, wwa, w0, a0, g_up, k_k, k_a, r_k, gn_g, gn_b)


def _ret_kernel(z_ref, pos_ref, freq_ref, mask_ref, xi_ref, zeta_ref, cd_ref, o_ref, r_ref):
    @pl.when(pl.program_id(1) == 0)
    def _():
        r_ref[...] = jnp.zeros_like(r_ref)

    c = RET_CHUNK
    ang = pos_ref[...] * freq_ref[...]
    lane = lax.broadcasted_iota(jnp.int32, (c, LANES), 1)
    cos2 = jnp.cos(ang)
    sin2 = jnp.where(lane < RET_QK_DIM // 2, -jnp.sin(ang), jnp.sin(ang))

    def rot(x):
        return x * cos2 + pltpu.roll(x, RET_QK_DIM // 2, axis=1) * sin2

    for h in range(RET_HEADS):
        q = rot(z_ref[0, :, h * RET_QK_DIM:(h + 1) * RET_QK_DIM])
        k = rot(z_ref[0, :, RET_QK_WIDTH + h * RET_QK_DIM:RET_QK_WIDTH + (h + 1) * RET_QK_DIM])
        k = k * (RET_QK_DIM ** -0.5)
        v0 = 2 * RET_QK_WIDTH + h * RET_V_DIM
        v = z_ref[0, :, v0:v0 + RET_V_DIM]
        g0 = 2 * RET_QK_WIDTH + RET_V_WIDTH + h * RET_V_DIM
        g = z_ref[0, :, g0:g0 + RET_V_DIM]

        scores = _dot_nt(q, k) * mask_ref[h]
        state = r_ref[h]
        y = _dot(scores, v) + _dot(q, state) * xi_ref[h]
        r_ref[h] = state * cd_ref[h] + _dot_tn(k * zeta_ref[h], v)

        mu = jnp.mean(y, -1, keepdims=True)
        yc = y - mu
        var = jnp.mean(yc * yc, -1, keepdims=True)
        o_ref[0, :, h * RET_V_DIM:(h + 1) * RET_V_DIM] = (
            g * jax.nn.sigmoid(g) * (yc * lax.rsqrt(var + LN_EPS)))


def _retention(z, pos, freq2, mask, xi_b, zeta_b, cd):
    b, s, _ = z.shape
    c = RET_CHUNK
    return pl.pallas_call(
        _ret_kernel,
        grid=(b, s // c),
        in_specs=[
            pl.BlockSpec((1, c, RET_IN), lambda i, j: (i, j, 0)),
            pl.BlockSpec((c, 1), lambda i, j, nc=s // c: (i * nc + j, 0)),
            _const_spec((1, LANES)),
            _const_spec((RET_HEADS, c, c)),
            _const_spec((RET_HEADS, c, RET_V_DIM)),
            _const_spec((RET_HEADS, c, RET_QK_DIM)),
            pl.BlockSpec(memory_space=pltpu.SMEM),
        ],
        out_specs=pl.BlockSpec((1, c, RET_V_WIDTH), lambda i, j: (i, j, 0)),
        out_shape=jax.ShapeDtypeStruct((b, s, RET_V_WIDTH), F32),
        scratch_shapes=[pltpu.VMEM((RET_HEADS, RET_QK_DIM, RET_V_DIM), F32)],
        compiler_params=pltpu.CompilerParams(
            dimension_semantics=("arbitrary", "arbitrary"), vmem_limit_bytes=VMEM_LIMIT),
        name="retention",
    )(z, pos, freq2, mask, xi_b, zeta_b, cd)


def _merge_kernel(yr_ref, yt_ref, zg_ref, h_ref, wbr_ref, wbt_ref, wo_ref, g_ref, b_ref, o_ref):
    a = _dot(yr_ref[...], wbr_ref[...])
    b = _dot(yt_ref[...], wbt_ref[...])
    m = zg_ref[:, :D_MODEL] * a + zg_ref[:, D_MODEL:] * b
    mix = _dot(m, wo_ref[...])
    o_ref[...] = _layer_norm(DN_ALPHA * h_ref[...] + mix, g_ref[...], b_ref[...])


def _merge(yr, yt, zg, h, wbr, wbt, wo, g, b, *, tm=512):
    n = h.shape[0]
    tok = lambda w: pl.BlockSpec((tm, w), lambda i: (i, 0))
    return pl.pallas_call(
        _merge_kernel,
        grid=(n // tm,),
        in_specs=[
            tok(RWKV_WIDTH), tok(RET_V_WIDTH), tok(GATE_IN), tok(D_MODEL),
            _const_spec((RWKV_WIDTH, D_MODEL)),
            _const_spec((RET_V_WIDTH, D_MODEL)),
            _const_spec((D_MODEL, D_MODEL)),
            _const_spec((1, D_MODEL)),
            _const_spec((1, D_MODEL)),
        ],
        out_specs=tok(D_MODEL),
        out_shape=jax.ShapeDtypeStruct((n, D_MODEL), F32),
        compiler_params=pltpu.CompilerParams(
            dimension_semantics=("parallel",), vmem_limit_bytes=VMEM_LIMIT),
        name="merge",
    )(yr, yt, zg, h, wbr, wbt, wo, g, b)


def _ple_kernel(h_ref, p_ref, wg_ref, wp_ref, o_ref):
    h = h_ref[...]
    gate = jax.nn.sigmoid(_dot(h, wg_ref[...]))
    o_ref[...] = h + gate * _dot(p_ref[...], wp_ref[...])


def _ple(h, p, wg, wp, *, tm=512):
    n = h.shape[0]
    return pl.pallas_call(
        _ple_kernel,
        grid=(n // tm,),
        in_specs=[
            pl.BlockSpec((tm, D_MODEL), lambda i: (i, 0)),
            pl.BlockSpec((tm, PLE_DIM), lambda i: (i, 0)),
            _const_spec((D_MODEL, D_MODEL)),
            _const_spec((PLE_DIM, D_MODEL)),
        ],
        out_specs=pl.BlockSpec((tm, D_MODEL), lambda i: (i, 0)),
        out_shape=jax.ShapeDtypeStruct((n, D_MODEL), F32),
        compiler_params=pltpu.CompilerParams(
            dimension_semantics=("parallel",), vmem_limit_bytes=VMEM_LIMIT),
        name="ple",
    )(h, p, wg, wp)


def _retention_constants():
    h, c = RET_HEADS, RET_CHUNK
    log_gamma = jnp.log(1.0 - jnp.exp2(-5.0 - jnp.arange(h, dtype=F32)))
    idx = jnp.arange(c, dtype=F32)
    rel = idx[:, None] - idx[None, :]
    mask = jnp.where(rel >= 0, jnp.exp(log_gamma[:, None, None] * jnp.maximum(rel, 0.0)), 0.0)
    xi = jnp.exp(log_gamma[:, None] * (idx + 1.0))
    zeta = jnp.exp(log_gamma[:, None] * (c - 1.0 - idx))
    cd = jnp.exp(log_gamma * c)
    xi_b = jnp.broadcast_to(xi[:, :, None], (h, c, RET_V_DIM))
    zeta_b = jnp.broadcast_to(zeta[:, :, None], (h, c, RET_QK_DIM))
    half = RET_QK_DIM // 2
    inv_freq = ROPE_BASE ** (-jnp.arange(half, dtype=F32) / half)
    freq2 = jnp.concatenate([inv_freq, inv_freq])[None, :]
    return freq2, mask, xi_b, zeta_b, cd


def kernel(x, p, positions, ln1_g, ln1_b, ffn1_w_in, ffn1_w_out, w_mix_in, rwkv_mu, rwkv_w0, rwkv_w_up,
           rwkv_a0, rwkv_a_up, rwkv_g_up, rwkv_k_k, rwkv_k_a, rwkv_r_k, rwkv_gn_g, rwkv_gn_b,
           w_branch_rwkv, w_branch_ret, w_mix_out, ln2_g, ln2_b, ffn2_w_in, ffn2_w_out, ln3_g, ln3_b,
           ple_w_proj, ple_w_gate):
    b, s, d = x.shape
    n = b * s
    bf = lambda w: w.astype(BF16)
    freq2, mask, xi_b, zeta_b, cd = _retention_constants()
    pos = positions.astype(F32).reshape(n, 1)
    h = x.reshape(n, d)
    for i in range(DEPTH):
        h = _ffn(h, bf(ffn1_w_in[i]), bf(ffn1_w_out[i]), ln1_g[i][None], ln1_b[i][None])
        zr, zt, zg = _mix_in(h, bf(w_mix_in[i]))
        zeros = jnp.zeros((DECAY_LORA, RWKV_WIDTH), F32)
        wwa = jnp.concatenate([
            jnp.concatenate([rwkv_w_up[i], zeros], axis=1),
            jnp.concatenate([zeros, rwkv_a_up[i]], axis=1)], axis=0)
        yr = _rwkv(zr.reshape(b, s, RWKV_IN), rwkv_mu[i][None], bf(wwa), rwkv_w0[i][None],
                   rwkv_a0[i][None], bf(rwkv_g_up[i]), rwkv_k_k[i][None], rwkv_k_a[i][None],
                   rwkv_r_k[i].reshape(1, RWKV_WIDTH), rwkv_gn_g[i][None], rwkv_gn_b[i][None])
        yt = _retention(zt.reshape(b, s, RET_IN), pos, freq2, mask, xi_b, zeta_b, cd)
        h = _merge(yr.reshape(n, RWKV_WIDTH), yt.reshape(n, RET_V_WIDTH), zg, h,
                   bf(w_branch_rwkv[i]), bf(w_branch_ret[i]), bf(w_mix_out[i]),
                   ln2_g[i][None], ln2_b[i][None])
        h = _ffn(h, bf(ffn2_w_in[i]), bf(ffn2_w_out[i]), ln3_g[i][None], ln3_b[i][None])
        h = _ple(h, p[i].reshape(n, PLE_DIM), bf(ple_w_gate[i]), bf(ple_w_proj[i]))
    return h.reshape(b, s, d)
```

```python
import functools
import math

import jax
import jax.numpy as jnp
from jax import lax
from jax.experimental import pallas as pl
from jax.experimental.pallas import tpu as pltpu

F32 = jnp.float32
BF16 = jnp.bfloat16

D_MODEL = 1024
PLE_DIM = 256
D_FF = 2816
RWKV_HEADS = 8
RWKV_HEAD_DIM = 64
RWKV_WIDTH = RWKV_HEADS * RWKV_HEAD_DIM
DECAY_LORA = 64
AAA_LORA = 64
GATE_LORA = 128
RWKV_GN_EPS = 64e-5
DECAY_SCALE = math.exp(-0.5)
RET_HEADS = 4
RET_QK_DIM = 128
RET_V_DIM = 256
RET_QK_WIDTH = RET_HEADS * RET_QK_DIM
RET_V_WIDTH = RET_HEADS * RET_V_DIM
RET_CHUNK = 128
ROPE_BASE = 10000.0
RWKV_IN = 3 * RWKV_WIDTH + DECAY_LORA + AAA_LORA + GATE_LORA
RET_IN = 2 * RET_QK_WIDTH + 2 * RET_V_WIDTH
GATE_IN = 2 * D_MODEL
DEPTH = 1
DN_ALPHA = (2 * DEPTH) ** 0.25
LN_EPS = 1e-5

LANES = 128
RWKV_CHUNK = 64
RWKV_PAIRS = RWKV_WIDTH // LANES
VMEM_LIMIT = 56 * 1024 * 1024


def _dot(a, b):
    return jnp.dot(a.astype(BF16), b.astype(BF16), preferred_element_type=F32)


def _dot_nt(a, b):
    return lax.dot_general(a.astype(BF16), b.astype(BF16), (((1,), (1,)), ((), ())),
                           preferred_element_type=F32)


def _dot_tn(a, b):
    return lax.dot_general(a.astype(BF16), b.astype(BF16), (((0,), (0,)), ((), ())),
                           preferred_element_type=F32)


def _bdot(a, b):
    return lax.dot_general(a.astype(BF16), b.astype(BF16), (((2,), (1,)), ((0,), (0,))),
                           preferred_element_type=F32)


def _bdot_nt(a, b):
    return lax.dot_general(a.astype(BF16), b.astype(BF16), (((2,), (2,)), ((0,), (0,))),
                           preferred_element_type=F32)


def _bdot_tn(a, b):
    return lax.dot_general(a.astype(BF16), b.astype(BF16), (((1,), (1,)), ((0,), (0,))),
                           preferred_element_type=F32)


def _dot_01(x, m01, *, left=False):
    hi = x.astype(BF16)
    r1 = x - hi.astype(F32)
    mid = r1.astype(BF16)
    lo = (r1 - mid.astype(F32)).astype(BF16)
    if left:
        dot = lambda t: jnp.dot(m01, t, preferred_element_type=F32)
    else:
        dot = lambda t: jnp.dot(t, m01, preferred_element_type=F32)
    return dot(hi) + dot(mid) + dot(lo)


def _layer_norm(y, g, b):
    mu = jnp.mean(y, -1, keepdims=True)
    yc = y - mu
    var = jnp.mean(yc * yc, -1, keepdims=True)
    return yc * lax.rsqrt(var + LN_EPS) * g + b


def _const_spec(shape):
    return pl.BlockSpec(shape, lambda *_: (0,) * len(shape))


def _ffn_kernel(x_ref, win_ref, wout_ref, g_ref, b_ref, o_ref, acc_ref, *, fc):
    x = x_ref[...]
    xb = x.astype(BF16)
    for j in range(D_FF // fc):
        gate = jnp.dot(xb, win_ref[:, j * fc:(j + 1) * fc], preferred_element_type=F32)
        up = jnp.dot(xb, win_ref[:, D_FF + j * fc:D_FF + (j + 1) * fc], preferred_element_type=F32)
        act = (gate * jax.nn.sigmoid(gate) * up).astype(BF16)
        part = jnp.dot(act, wout_ref[j * fc:(j + 1) * fc, :], preferred_element_type=F32)
        if j == 0:
            acc_ref[...] = part
        else:
            acc_ref[...] += part
    y = DN_ALPHA * x + 0.5 * acc_ref[...]
    o_ref[...] = _layer_norm(y, g_ref[...], b_ref[...])


def _ffn(x, w_in, w_out, g, b, *, tm=512, fc=256):
    n = x.shape[0]
    return pl.pallas_call(
        functools.partial(_ffn_kernel, fc=fc),
        grid=(n // tm,),
        in_specs=[
            pl.BlockSpec((tm, D_MODEL), lambda i: (i, 0)),
            _const_spec((D_MODEL, 2 * D_FF)),
            _const_spec((D_FF, D_MODEL)),
            _const_spec((1, D_MODEL)),
            _const_spec((1, D_MODEL)),
        ],
        out_specs=pl.BlockSpec((tm, D_MODEL), lambda i: (i, 0)),
        out_shape=jax.ShapeDtypeStruct((n, D_MODEL), F32),
        scratch_shapes=[pltpu.VMEM((tm, D_MODEL), F32)],
        compiler_params=pltpu.CompilerParams(
            dimension_semantics=("parallel",), vmem_limit_bytes=VMEM_LIMIT),
        name="ffn",
    )(x, w_in, w_out, g, b)


def _mixin_kernel(h_ref, w_ref, zr_ref, zt_ref, zg_ref):
    hb = h_ref[...].astype(BF16)
    zr_ref[...] = jnp.dot(hb, w_ref[:, :RWKV_IN], preferred_element_type=F32)
    zt_ref[...] = jnp.dot(hb, w_ref[:, RWKV_IN:RWKV_IN + RET_IN], preferred_element_type=F32)
    zg_ref[...] = jax.nn.sigmoid(
        jnp.dot(hb, w_ref[:, RWKV_IN + RET_IN:], preferred_element_type=F32))


def _mix_in(h, w, *, tm=256):
    n = h.shape[0]
    return pl.pallas_call(
        _mixin_kernel,
        grid=(n // tm,),
        in_specs=[
            pl.BlockSpec((tm, D_MODEL), lambda i: (i, 0)),
            _const_spec((D_MODEL, RWKV_IN + RET_IN + GATE_IN)),
        ],
        out_specs=[
            pl.BlockSpec((tm, RWKV_IN), lambda i: (i, 0)),
            pl.BlockSpec((tm, RET_IN), lambda i: (i, 0)),
            pl.BlockSpec((tm, GATE_IN), lambda i: (i, 0)),
        ],
        out_shape=[
            jax.ShapeDtypeStruct((n, RWKV_IN), F32),
            jax.ShapeDtypeStruct((n, RET_IN), F32),
            jax.ShapeDtypeStruct((n, GATE_IN), F32),
        ],
        compiler_params=pltpu.CompilerParams(
            dimension_semantics=("parallel",), vmem_limit_bytes=VMEM_LIMIT),
        name="mix_in",
    )(h, w)


def _rwkv_kernel(z_ref, mu_ref, wwa_ref, w0_ref, a0_ref, gup_ref, kk_ref, ka_ref, rk_ref,
                 gng_ref, gnb_ref, o_ref, carry_ref, t_ref, *, tb):
    L = RWKV_CHUNK
    W = RWKV_WIDTH
    nc = tb // L

    @pl.when(pl.program_id(1) == 0)
    def _():
        carry_ref[...] = jnp.zeros_like(carry_ref)
        t_ref[...] = jnp.zeros_like(t_ref)

    z = z_ref[0]
    row = lax.broadcasted_iota(jnp.int32, z.shape, 0)
    prev = jnp.where(row == 0, carry_ref[...], pltpu.roll(z, 1, axis=0))
    carry_ref[...] = z[tb - 1:tb, :]
    zs = z + (prev - z) * mu_ref[...]

    r = zs[:, :W]
    k = zs[:, W:2 * W]
    v = zs[:, 2 * W:3 * W]
    dwa = zs[:, 3 * W:3 * W + LANES]
    dg = zs[:, 3 * W + LANES:]
    lane = lax.broadcasted_iota(jnp.int32, (tb, LANES), 1)
    dwa = jnp.where(lane < RWKV_HEAD_DIM, jnp.tanh(dwa), dwa)
    wa = _dot(dwa, wwa_ref[...])
    lw = -DECAY_SCALE * jax.nn.sigmoid(w0_ref[...] + wa[:, :W])
    a = jax.nn.sigmoid(a0_ref[...] + wa[:, W:])
    g = _dot(jax.nn.sigmoid(dg), gup_ref[...])

    jr = lax.broadcasted_iota(jnp.int32, (LANES, LANES), 0)
    jc = lax.broadcasted_iota(jnp.int32, (LANES, LANES), 1)
    same_head = (jr < RWKV_HEAD_DIM) == (jc < RWKV_HEAD_DIM)
    ones_bd = jnp.where(same_head, 1.0, 0.0).astype(BF16)

    def head_sum(x):
        return jnp.concatenate(
            [_dot_01(x[:, p * LANES:(p + 1) * LANES], ones_bd) for p in range(RWKV_PAIRS)], axis=1)

    kk = k * kk_ref[...]
    kk = kk * lax.rsqrt(jnp.maximum(head_sum(kk * kk), 1e-24))
    k2 = k * (1.0 + (a - 1.0) * ka_ref[...])
    bonus = head_sum(r * k2 * rk_ref[...]) * v

    tr = lax.broadcasted_iota(jnp.int32, (tb, tb), 0)
    tc = lax.broadcasted_iota(jnp.int32, (tb, tb), 1)
    chunk_shift = L.bit_length() - 1
    tril = jnp.where(((tr >> chunk_shift) == (tc >> chunk_shift)) & (tc <= tr), 1.0, 0.0).astype(BF16)
    cs = _dot_01(lw, tril, left=True)
    gl = jnp.concatenate(
        [jnp.broadcast_to(cs[c * L + L - 1:c * L + L, :], (L, W)) for c in range(nc)], axis=0)

    def tiles(x):
        return jnp.stack([x[c * L:(c + 1) * L, p * LANES:(p + 1) * LANES]
                          for c in range(nc) for p in range(RWKV_PAIRS)])

    ka = kk * a
    inv = jnp.exp(-cs)
    end = jnp.exp(gl - cs)
    aq = tiles((kk * jnp.exp(cs - lw)).astype(BF16))
    rq32 = tiles(r * jnp.exp(cs))
    rq = rq32.astype(BF16)
    kd = tiles((k2 * inv).astype(BF16))
    bd = tiles((ka * inv).astype(BF16))
    ke = tiles((k2 * end).astype(BF16))
    be = tiles((ka * end).astype(BF16))
    vv = tiles(v.astype(BF16))
    gam_l = tiles(jnp.exp(gl))[:, 0:1, :]

    trow = lax.broadcasted_iota(jnp.int32, (L, LANES), 0)
    scol = lax.broadcasted_iota(jnp.int32, (L, LANES), 1)
    m_left = scol < RWKV_HEAD_DIM
    scol = scol & (L - 1)
    strict = scol < trow
    incl = scol <= trow
    eye_pair = jnp.where(scol == trow, 1.0, 0.0).astype(F32)
    diag = jr == jc

    def bd2(x):
        zero = jnp.zeros_like(x)
        return jnp.concatenate([jnp.where(m_left, x, zero), jnp.where(m_left, zero, x)], axis=1)

    m = _bdot_nt(jnp.concatenate([aq, rq], axis=1),
                 jnp.concatenate([bd2(kd), bd2(bd)], axis=1))
    mak = jnp.where(strict, m[:, :L, :LANES], 0.0).astype(BF16)
    mab = jnp.where(strict, m[:, :L, LANES:], 0.0)
    mrk = jnp.where(incl, m[:, L:, :LANES], 0.0).astype(BF16)
    mrb = jnp.where(incl, m[:, L:, LANES:], 0.0).astype(BF16)

    npow = mab.astype(BF16)
    pinv = eye_pair - mab
    for _ in range(5):
        npow = _bdot(npow, bd2(npow)).astype(BF16)
        pinv = pinv + _bdot(pinv, bd2(npow))

    makv = _bdot(mak, bd2(vv)).astype(BF16)
    au = _bdot(pinv, jnp.concatenate([bd2(aq), bd2(makv)], axis=2)).astype(BF16)
    ahat = au[:, :, :LANES]
    uhat = au[:, :, LANES:]
    qhat = rq32 - _bdot(mrb, bd2(ahat))
    yhat = _bdot(jnp.concatenate([mrk, -mrb], axis=2),
                 jnp.concatenate([bd2(vv), bd2(uhat)], axis=1))
    bta = _bdot_tn(be, au)
    ktv = _bdot_tn(ke, vv)
    gmat = jnp.where(same_head, jnp.where(diag, gam_l, 0.0) - bta[:, :, :LANES], 0.0)
    hmat = jnp.where(same_head, ktv - bta[:, :, LANES:], 0.0)

    state = t_ref[...]
    y_rows = []
    for c in range(nc):
        sel = slice(c * RWKV_PAIRS, (c + 1) * RWKV_PAIRS)
        yc = _bdot(qhat[sel], state) + yhat[sel]
        y_rows.append(jnp.concatenate([yc[p] for p in range(RWKV_PAIRS)], axis=1))
        state = _bdot(gmat[sel], state) + hmat[sel]
    t_ref[...] = state

    y = jnp.concatenate(y_rows, axis=0)
    mean = head_sum(y) * (1.0 / RWKV_HEAD_DIM)
    yc = y - mean
    var = head_sum(yc * yc) * (1.0 / RWKV_HEAD_DIM)
    yn = yc * lax.rsqrt(var + RWKV_GN_EPS) * gng_ref[...] + gnb_ref[...]
    o_ref[0] = (yn + bonus) * g


def _rwkv(z, mu, wwa, w0, a0, g_up, k_k, k_a, r_k, gn_g, gn_b, *, tb=256):
    b, s, _ = z.shape
    W = RWKV_WIDTH
    return pl.pallas_call(
        functools.partial(_rwkv_kernel, tb=tb),
        grid=(b, s // tb),
        in_specs=[
            pl.BlockSpec((1, tb, RWKV_IN), lambda i, j: (i, j, 0)),
            _const_spec((1, RWKV_IN)),
            _const_spec((LANES, 2 * W)),
            _const_spec((1, W)),
            _const_spec((1, W)),
            _const_spec((GATE_LORA, W)),
            _const_spec((1, W)),
            _const_spec((1, W)),
            _const_spec((1, W)),
            _const_spec((1, W)),
            _const_spec((1, W)),
        ],
        out_specs=pl.BlockSpec((1, tb, W), lambda i, j: (i, j, 0)),
        out_shape=jax.ShapeDtypeStruct((b, s, W), F32),
        scratch_shapes=[
            pltpu.VMEM((1, RWKV_IN), F32),
            pltpu.VMEM((RWKV_PAIRS, LANES, LANES), F32),
        ],
        compiler_params=pltpu.CompilerParams(
            dimension_semantics=("arbitrary", "arbitrary"), vmem_limit_bytes=VMEM_LIMIT),
        name="rwkv",
    )(z, mu, wwa, w0, a0, g_up, k_k, k_a, r_k, gn_g, gn_b)


def _ret_kernel(z_ref, pos_ref, freq_ref, mask_ref, xi_ref, zeta_ref, cd_ref, o_ref, r_ref):
    @pl.when(pl.program_id(1) == 0)
    def _():
        r_ref[...] = jnp.zeros_like(r_ref)

    c = RET_CHUNK
    ang = pos_ref[...] * freq_ref[...]
    lane = lax.broadcasted_iota(jnp.int32, (c, LANES), 1)
    cos2 = jnp.cos(ang)
    sin2 = jnp.where(lane < RET_QK_DIM // 2, -jnp.sin(ang), jnp.sin(ang))

    def rot(x):
        return x * cos2 + pltpu.roll(x, RET_QK_DIM // 2, axis=1) * sin2

    for h in range(RET_HEADS):
        q = rot(z_ref[0, :, h * RET_QK_DIM:(h + 1) * RET_QK_DIM])
        k = rot(z_ref[0, :, RET_QK_WIDTH + h * RET_QK_DIM:RET_QK_WIDTH + (h + 1) * RET_QK_DIM])
        k = k * (RET_QK_DIM ** -0.5)
        v0 = 2 * RET_QK_WIDTH + h * RET_V_DIM
        v = z_ref[0, :, v0:v0 + RET_V_DIM]
        g0 = 2 * RET_QK_WIDTH + RET_V_WIDTH + h * RET_V_DIM
        g = z_ref[0, :, g0:g0 + RET_V_DIM]

        scores = _dot_nt(q, k) * mask_ref[h]
        state = r_ref[h]
        y = _dot(scores, v) + _dot(q, state) * xi_ref[h]
        r_ref[h] = state * cd_ref[h] + _dot_tn(k * zeta_ref[h], v)

        mu = jnp.mean(y, -1, keepdims=True)
        yc = y - mu
        var = jnp.mean(yc * yc, -1, keepdims=True)
        o_ref[0, :, h * RET_V_DIM:(h + 1) * RET_V_DIM] = (
            g * jax.nn.sigmoid(g) * (yc * lax.rsqrt(var + LN_EPS)))


def _retention(z, pos, freq2, mask, xi_b, zeta_b, cd):
    b, s, _ = z.shape
    c = RET_CHUNK
    return pl.pallas_call(
        _ret_kernel,
        grid=(b, s // c),
        in_specs=[
            pl.BlockSpec((1, c, RET_IN), lambda i, j: (i, j, 0)),
            pl.BlockSpec((c, 1), lambda i, j, nc=s // c: (i * nc + j, 0)),
            _const_spec((1, LANES)),
            _const_spec((RET_HEADS, c, c)),
            _const_spec((RET_HEADS, c, RET_V_DIM)),
            _const_spec((RET_HEADS, c, RET_QK_DIM)),
            pl.BlockSpec(memory_space=pltpu.SMEM),
        ],
        out_specs=pl.BlockSpec((1, c, RET_V_WIDTH), lambda i, j: (i, j, 0)),
        out_shape=jax.ShapeDtypeStruct((b, s, RET_V_WIDTH), F32),
        scratch_shapes=[pltpu.VMEM((RET_HEADS, RET_QK_DIM, RET_V_DIM), F32)],
        compiler_params=pltpu.CompilerParams(
            dimension_semantics=("arbitrary", "arbitrary"), vmem_limit_bytes=VMEM_LIMIT),
        name="retention",
    )(z, pos, freq2, mask, xi_b, zeta_b, cd)


def _merge_kernel(yr_ref, yt_ref, zg_ref, h_ref, wbr_ref, wbt_ref, wo_ref, g_ref, b_ref, o_ref):
    a = _dot(yr_ref[...], wbr_ref[...])
    b = _dot(yt_ref[...], wbt_ref[...])
    m = zg_ref[:, :D_MODEL] * a + zg_ref[:, D_MODEL:] * b
    mix = _dot(m, wo_ref[...])
    o_ref[...] = _layer_norm(DN_ALPHA * h_ref[...] + mix, g_ref[...], b_ref[...])


def _merge(yr, yt, zg, h, wbr, wbt, wo, g, b, *, tm=512):
    n = h.shape[0]
    tok = lambda w: pl.BlockSpec((tm, w), lambda i: (i, 0))
    return pl.pallas_call(
        _merge_kernel,
        grid=(n // tm,),
        in_specs=[
            tok(RWKV_WIDTH), tok(RET_V_WIDTH), tok(GATE_IN), tok(D_MODEL),
            _const_spec((RWKV_WIDTH, D_MODEL)),
            _const_spec((RET_V_WIDTH, D_MODEL)),
            _const_spec((D_MODEL, D_MODEL)),
            _const_spec((1, D_MODEL)),
            _const_spec((1, D_MODEL)),
        ],
        out_specs=tok(D_MODEL),
        out_shape=jax.ShapeDtypeStruct((n, D_MODEL), F32),
        compiler_params=pltpu.CompilerParams(
            dimension_semantics=("parallel",), vmem_limit_bytes=VMEM_LIMIT),
        name="merge",
    )(yr, yt, zg, h, wbr, wbt, wo, g, b)


def _ple_kernel(h_ref, p_ref, wg_ref, wp_ref, o_ref):
    h = h_ref[...]
    gate = jax.nn.sigmoid(_dot(h, wg_ref[...]))
    o_ref[...] = h + gate * _dot(p_ref[...], wp_ref[...])


def _ple(h, p, wg, wp, *, tm=512):
    n = h.shape[0]
    return pl.pallas_call(
        _ple_kernel,
        grid=(n // tm,),
        in_specs=[
            pl.BlockSpec((tm, D_MODEL), lambda i: (i, 0)),
            pl.BlockSpec((tm, PLE_DIM), lambda i: (i, 0)),
            _const_spec((D_MODEL, D_MODEL)),
            _const_spec((PLE_DIM, D_MODEL)),
        ],
        out_specs=pl.BlockSpec((tm, D_MODEL), lambda i: (i, 0)),
        out_shape=jax.ShapeDtypeStruct((n, D_MODEL), F32),
        compiler_params=pltpu.CompilerParams(
            dimension_semantics=("parallel",), vmem_limit_bytes=VMEM_LIMIT),
        name="ple",
    )(h, p, wg, wp)


def _retention_constants():
    h, c = RET_HEADS, RET_CHUNK
    log_gamma = jnp.log(1.0 - jnp.exp2(-5.0 - jnp.arange(h, dtype=F32)))
    idx = jnp.arange(c, dtype=F32)
    rel = idx[:, None] - idx[None, :]
    mask = jnp.where(rel >= 0, jnp.exp(log_gamma[:, None, None] * jnp.maximum(rel, 0.0)), 0.0)
    xi = jnp.exp(log_gamma[:, None] * (idx + 1.0))
    zeta = jnp.exp(log_gamma[:, None] * (c - 1.0 - idx))
    cd = jnp.exp(log_gamma * c)
    xi_b = jnp.broadcast_to(xi[:, :, None], (h, c, RET_V_DIM))
    zeta_b = jnp.broadcast_to(zeta[:, :, None], (h, c, RET_QK_DIM))
    half = RET_QK_DIM // 2
    inv_freq = ROPE_BASE ** (-jnp.arange(half, dtype=F32) / half)
    freq2 = jnp.concatenate([inv_freq, inv_freq])[None, :]
    return freq2, mask, xi_b, zeta_b, cd


def kernel(x, p, positions, ln1_g, ln1_b, ffn1_w_in, ffn1_w_out, w_mix_in, rwkv_mu, rwkv_w0, rwkv_w_up,
           rwkv_a0, rwkv_a_up, rwkv_g_up, rwkv_k_k, rwkv_k_a, rwkv_r_k, rwkv_gn_g, rwkv_gn_b,
           w_branch_rwkv, w_branch_ret, w_mix_out, ln2_g, ln2_b, ffn2_w_in, ffn2_w_out, ln3_g, ln3_b,
           ple_w_proj, ple_w_gate):
    b, s, d = x.shape
    n = b * s
    bf = lambda w: w.astype(BF16)
    freq2, mask, xi_b, zeta_b, cd = _retention_constants()
    pos = positions.astype(F32).reshape(n, 1)
    h = x.reshape(n, d)
    for i in range(DEPTH):
        h = _ffn(h, bf(ffn1_w_in[i]), bf(ffn1_w_out[i]), ln1_g[i][None], ln1_b[i][None])
        zr, zt, zg = _mix_in(h, bf(w_mix_in[i]))
        zeros = jnp.zeros((DECAY_LORA, RWKV_WIDTH), F32)
        wwa = jnp.concatenate([
            jnp.concatenate([rwkv_w_up[i], zeros], axis=1),
            jnp.concatenate([zeros, rwkv_a_up[i]], axis=1)], axis=0)
        yr = _rwkv(zr.reshape(b, s, RWKV_IN), rwkv_mu[i][None], bf(wwa), rwkv_w0[i][None],
                   rwkv_a0[i][None], bf(rwkv_g_up[i]), rwkv_k_k[i][None], rwkv_k_a[i][None],
                   rwkv_r_k[i].reshape(1, RWKV_WIDTH), rwkv_gn_g[i][None], rwkv_gn_b[i][None])
        yt = _retention(zt.reshape(b, s, RET_IN), pos, freq2, mask, xi_b, zeta_b, cd)
        h = _merge(yr.reshape(n, RWKV_WIDTH), yt.reshape(n, RET_V_WIDTH), zg, h,
                   bf(w_branch_rwkv[i]), bf(w_branch_ret[i]), bf(w_mix_out[i]),
                   ln2_g[i][None], ln2_b[i][None])
        h = _ffn(h, bf(ffn2_w_in[i]), bf(ffn2_w_out[i]), ln3_g[i][None], ln3_b[i][None])
        h = _ple(h, p[i].reshape(n, PLE_DIM), bf(ple_w_gate[i]), bf(ple_w_proj[i]))
    return h.reshape(b, s, d)
```

```python
import functools
import math

import jax
import jax.numpy as jnp
from jax import lax
from jax.experimental import pallas as pl
from jax.experimental.pallas import tpu as pltpu

F32 = jnp.float32
BF16 = jnp.bfloat16

D_MODEL = 1024
PLE_DIM = 256
D_FF = 2816
RWKV_HEADS = 8
RWKV_HEAD_DIM = 64
RWKV_WIDTH = RWKV_HEADS * RWKV_HEAD_DIM
DECAY_LORA = 64
AAA_LORA = 64
GATE_LORA = 128
RWKV_GN_EPS = 64e-5
DECAY_SCALE = math.exp(-0.5)
RET_HEADS = 4
RET_QK_DIM = 128
RET_V_DIM = 256
RET_QK_WIDTH = RET_HEADS * RET_QK_DIM
RET_V_WIDTH = RET_HEADS * RET_V_DIM
RET_CHUNK = 128
ROPE_BASE = 10000.0
RWKV_IN = 3 * RWKV_WIDTH + DECAY_LORA + AAA_LORA + GATE_LORA
RET_IN = 2 * RET_QK_WIDTH + 2 * RET_V_WIDTH
GATE_IN = 2 * D_MODEL
DEPTH = 1
DN_ALPHA = (2 * DEPTH) ** 0.25
LN_EPS = 1e-5

LANES = 128
RWKV_CHUNK = 64
RWKV_PAIRS = RWKV_WIDTH // LANES
VMEM_LIMIT = 56 * 1024 * 1024


def _dot(a, b):
    return jnp.dot(a.astype(BF16), b.astype(BF16), preferred_element_type=F32)


def _dot_nt(a, b):
    return lax.dot_general(a.astype(BF16), b.astype(BF16), (((1,), (1,)), ((), ())),
                           preferred_element_type=F32)


def _dot_tn(a, b):
    return lax.dot_general(a.astype(BF16), b.astype(BF16), (((0,), (0,)), ((), ())),
                           preferred_element_type=F32)


def _bdot(a, b):
    return lax.dot_general(a.astype(BF16), b.astype(BF16), (((2,), (1,)), ((0,), (0,))),
                           preferred_element_type=F32)


def _bdot_nt(a, b):
    return lax.dot_general(a.astype(BF16), b.astype(BF16), (((2,), (2,)), ((0,), (0,))),
                           preferred_element_type=F32)


def _bdot_tn(a, b):
    return lax.dot_general(a.astype(BF16), b.astype(BF16), (((1,), (1,)), ((0,), (0,))),
                           preferred_element_type=F32)


def _dot_01(x, m01, *, left=False):
    hi = x.astype(BF16)
    lo = (x - hi.astype(F32)).astype(BF16)
    if left:
        dot = lambda t: jnp.dot(m01, t, preferred_element_type=F32)
    else:
        dot = lambda t: jnp.dot(t, m01, preferred_element_type=F32)
    return dot(hi) + dot(lo)


def _layer_norm(y, g, b):
    mu = jnp.mean(y, -1, keepdims=True)
    yc = y - mu
    var = jnp.mean(yc * yc, -1, keepdims=True)
    return yc * lax.rsqrt(var + LN_EPS) * g + b


def _const_spec(shape):
    return pl.BlockSpec(shape, lambda *_: (0,) * len(shape), pipeline_mode=pl.Buffered(1))


def _ffn_block(x, win_ref, wout_ref, g_ref, b_ref, acc_ref, fc):
    xb = x.astype(BF16)
    for j in range(D_FF // fc):
        gate = jnp.dot(xb, win_ref[:, j * fc:(j + 1) * fc], preferred_element_type=F32)
        up = jnp.dot(xb, win_ref[:, D_FF + j * fc:D_FF + (j + 1) * fc], preferred_element_type=F32)
        act = (gate * jax.nn.sigmoid(gate) * up).astype(BF16)
        part = jnp.dot(act, wout_ref[j * fc:(j + 1) * fc, :], preferred_element_type=F32)
        if j == 0:
            acc_ref[...] = part
        else:
            acc_ref[...] += part
    y = DN_ALPHA * x + 0.5 * acc_ref[...]
    return _layer_norm(y, g_ref[...], b_ref[...])


def _ffn_kernel(x_ref, win_ref, wout_ref, g_ref, b_ref, o_ref, acc_ref, *, fc):
    o_ref[...] = _ffn_block(x_ref[...], win_ref, wout_ref, g_ref, b_ref, acc_ref, fc)


def _ffn(x, w_in, w_out, g, b, *, tm=512, fc=256):
    n = x.shape[0]
    return pl.pallas_call(
        functools.partial(_ffn_kernel, fc=fc),
        grid=(n // tm,),
        in_specs=[
            pl.BlockSpec((tm, D_MODEL), lambda i: (i, 0)),
            _const_spec((D_MODEL, 2 * D_FF)),
            _const_spec((D_FF, D_MODEL)),
            _const_spec((1, D_MODEL)),
            _const_spec((1, D_MODEL)),
        ],
        out_specs=pl.BlockSpec((tm, D_MODEL), lambda i: (i, 0)),
        out_shape=jax.ShapeDtypeStruct((n, D_MODEL), F32),
        scratch_shapes=[pltpu.VMEM((tm, D_MODEL), F32)],
        compiler_params=pltpu.CompilerParams(
            dimension_semantics=("parallel",), vmem_limit_bytes=VMEM_LIMIT),
        name="ffn",
    )(x, w_in, w_out, g, b)


def _mixin_kernel(h_ref, w_ref, zr_ref, zt_ref, zg_ref):
    hb = h_ref[...].astype(BF16)
    zr_ref[...] = jnp.dot(hb, w_ref[:, :RWKV_IN], preferred_element_type=F32)
    zt_ref[...] = jnp.dot(hb, w_ref[:, RWKV_IN:RWKV_IN + RET_IN], preferred_element_type=F32)
    zg_ref[...] = jax.nn.sigmoid(
        jnp.dot(hb, w_ref[:, RWKV_IN + RET_IN:], preferred_element_type=F32))


def _mix_in(h, w, *, tm=512):
    n = h.shape[0]
    return pl.pallas_call(
        _mixin_kernel,
        grid=(n // tm,),
        in_specs=[
            pl.BlockSpec((tm, D_MODEL), lambda i: (i, 0)),
            _const_spec((D_MODEL, RWKV_IN + RET_IN + GATE_IN)),
        ],
        out_specs=[
            pl.BlockSpec((tm, RWKV_IN), lambda i: (i, 0)),
            pl.BlockSpec((tm, RET_IN), lambda i: (i, 0)),
            pl.BlockSpec((tm, GATE_IN), lambda i: (i, 0)),
        ],
        out_shape=[
            jax.ShapeDtypeStruct((n, RWKV_IN), F32),
            jax.ShapeDtypeStruct((n, RET_IN), F32),
            jax.ShapeDtypeStruct((n, GATE_IN), F32),
        ],
        compiler_params=pltpu.CompilerParams(
            dimension_semantics=("parallel",), vmem_limit_bytes=VMEM_LIMIT),
        name="mix_in",
    )(h, w)


def _rwkv_body(z_ref, tril_ref, mu_ref, wwa_ref, w0_ref, a0_ref, gup_ref, kk_ref, ka_ref, rk_ref,
                 gng_ref, gnb_ref, o_ref, carry_ref, t_ref, *, tb):
    L = RWKV_CHUNK
    W = RWKV_WIDTH
    nc = tb // L

    z = z_ref[0]
    row = lax.broadcasted_iota(jnp.int32, z.shape, 0)
    prev = jnp.where(row == 0, carry_ref[...], pltpu.roll(z, 1, axis=0))
    carry_ref[...] = z[tb - 1:tb, :]
    zs = z + (prev - z) * mu_ref[...]

    r = zs[:, :W]
    k = zs[:, W:2 * W]
    v = zs[:, 2 * W:3 * W]
    dwa = zs[:, 3 * W:3 * W + LANES]
    dg = zs[:, 3 * W + LANES:]
    lane = lax.broadcasted_iota(jnp.int32, (tb, LANES), 1)
    dwa = jnp.where(lane < RWKV_HEAD_DIM, jnp.tanh(dwa), dwa)
    wa = _dot(dwa, wwa_ref[...])
    lw = -DECAY_SCALE * jax.nn.sigmoid(w0_ref[...] + wa[:, :W])
    a = jax.nn.sigmoid(a0_ref[...] + wa[:, W:])
    g = _dot(jax.nn.sigmoid(dg), gup_ref[...])

    jr = lax.broadcasted_iota(jnp.int32, (LANES, LANES), 0)
    jc = lax.broadcasted_iota(jnp.int32, (LANES, LANES), 1)
    same_head = (jr < RWKV_HEAD_DIM) == (jc < RWKV_HEAD_DIM)
    ones_bd = jnp.where(same_head, 1.0, 0.0).astype(BF16)

    def head_sum(x):
        return jnp.concatenate(
            [_dot_01(x[:, p * LANES:(p + 1) * LANES], ones_bd) for p in range(RWKV_PAIRS)], axis=1)

    kk = k * kk_ref[...]
    kk = kk * lax.rsqrt(jnp.maximum(head_sum(kk * kk), 1e-24))
    k2 = k * (1.0 + (a - 1.0) * ka_ref[...])
    bonus = head_sum(r * k2 * rk_ref[...]) * v

    cs = _dot_01(lw, tril_ref[...], left=True)
    gl = jnp.concatenate(
        [jnp.broadcast_to(cs[c * L + L - 1:c * L + L, :], (L, W)) for c in range(nc)], axis=0)

    def tiles(x):
        return jnp.stack([x[c * L:(c + 1) * L, p * LANES:(p + 1) * LANES]
                          for c in range(nc) for p in range(RWKV_PAIRS)])

    ka = kk * a
    inv = jnp.exp(-cs)
    end = jnp.exp(gl - cs)
    aq = tiles((kk * jnp.exp(cs - lw)).astype(BF16))
    rq32 = tiles(r * jnp.exp(cs))
    rq = rq32.astype(BF16)
    kd = tiles((k2 * inv).astype(BF16))
    bd = tiles((ka * inv).astype(BF16))
    ke = tiles((k2 * end).astype(BF16))
    be = tiles((ka * end).astype(BF16))
    vv = tiles(v.astype(BF16))
    gam_l = tiles(jnp.exp(gl))[:, 0:1, :]

    trow = lax.broadcasted_iota(jnp.int32, (L, LANES), 0)
    scol = lax.broadcasted_iota(jnp.int32, (L, LANES), 1)
    m_left = scol < RWKV_HEAD_DIM
    scol = scol & (L - 1)
    strict = scol < trow
    incl = scol <= trow
    eye_pair = jnp.where(scol == trow, 1.0, 0.0).astype(F32)
    diag = jr == jc

    def bd2(x):
        zero = jnp.zeros_like(x)
        return jnp.concatenate([jnp.where(m_left, x, zero), jnp.where(m_left, zero, x)], axis=1)

    m = _bdot_nt(jnp.concatenate([aq, rq], axis=1),
                 jnp.concatenate([bd2(kd), bd2(bd)], axis=1))
    mak = jnp.where(strict, m[:, :L, :LANES], 0.0).astype(BF16)
    mab = jnp.where(strict, m[:, :L, LANES:], 0.0)
    mrk = jnp.where(incl, m[:, L:, :LANES], 0.0).astype(BF16)
    mrb = jnp.where(incl, m[:, L:, LANES:], 0.0).astype(BF16)

    npow = mab.astype(BF16)
    pinv = eye_pair - mab
    for _ in range(5):
        npow = _bdot(npow, bd2(npow)).astype(BF16)
        pinv = pinv + _bdot(pinv, bd2(npow))

    makv = _bdot(mak, bd2(vv)).astype(BF16)
    au = _bdot(pinv, jnp.concatenate([bd2(aq), bd2(makv)], axis=2)).astype(BF16)
    ahat = au[:, :, :LANES]
    uhat = au[:, :, LANES:]
    qhat = rq32 - _bdot(mrb, bd2(ahat))
    yhat = _bdot(jnp.concatenate([mrk, -mrb], axis=2),
                 jnp.concatenate([bd2(vv), bd2(uhat)], axis=1))
    bta = _bdot_tn(be, au)
    ktv = _bdot_tn(ke, vv)
    gmat = jnp.where(same_head, jnp.where(diag, gam_l, 0.0) - bta[:, :, :LANES], 0.0)
    hmat = jnp.where(same_head, ktv - bta[:, :, LANES:], 0.0)

    state = t_ref[...]
    y_rows = []
    for c in range(nc):
        sel = slice(c * RWKV_PAIRS, (c + 1) * RWKV_PAIRS)
        yc = _bdot(qhat[sel], state) + yhat[sel]
        y_rows.append(jnp.concatenate([yc[p] for p in range(RWKV_PAIRS)], axis=1))
        state = _bdot(gmat[sel], state) + hmat[sel]
    t_ref[...] = state

    y = jnp.concatenate(y_rows, axis=0)
    mean = head_sum(y) * (1.0 / RWKV_HEAD_DIM)
    yc = y - mean
    var = head_sum(yc * yc) * (1.0 / RWKV_HEAD_DIM)
    yn = yc * lax.rsqrt(var + RWKV_GN_EPS) * gng_ref[...] + gnb_ref[...]
    o_ref[0] = (yn + bonus) * g


def _ret_body(z_ref, pos_ref, freq_ref, mask_ref, xi_ref, zeta_ref, cd_ref, o_ref, r_ref, *, tb):
    c = RET_CHUNK
    ang = pos_ref[...] * freq_ref[...]
    lane = lax.broadcasted_iota(jnp.int32, (tb, LANES), 1)
    cos2 = jnp.cos(ang)
    sin2 = jnp.sin(ang)
    sin2 = jnp.where(lane < RET_QK_DIM // 2, -sin2, sin2)

    def rot(x):
        return x * cos2 + pltpu.roll(x, RET_QK_DIM // 2, axis=1) * sin2

    for h in range(RET_HEADS):
        q = rot(z_ref[0, :, h * RET_QK_DIM:(h + 1) * RET_QK_DIM])
        k = rot(z_ref[0, :, RET_QK_WIDTH + h * RET_QK_DIM:RET_QK_WIDTH + (h + 1) * RET_QK_DIM])
        k = k * (RET_QK_DIM ** -0.5)
        v0 = 2 * RET_QK_WIDTH + h * RET_V_DIM
        g0 = 2 * RET_QK_WIDTH + RET_V_WIDTH + h * RET_V_DIM
        state = r_ref[h]
        for j in range(tb // c):
            rows = slice(j * c, (j + 1) * c)
            qj = q[rows]
            kj = k[rows]
            vj = z_ref[0, rows, v0:v0 + RET_V_DIM].astype(BF16)
            gj = z_ref[0, rows, g0:g0 + RET_V_DIM]
            scores = _dot_nt(qj, kj) * mask_ref[h]
            lhs = jnp.concatenate([scores.astype(BF16), (qj * xi_ref[h]).astype(BF16)], axis=1)
            rhs = jnp.concatenate([vj, state.astype(BF16)], axis=0)
            y = jnp.dot(lhs, rhs, preferred_element_type=F32)
            state = state * cd_ref[h] + _dot_tn(kj * zeta_ref[h], vj)

            mu = jnp.mean(y, -1, keepdims=True)
            yc = y - mu
            var = jnp.mean(yc * yc, -1, keepdims=True)
            o_ref[0, rows, h * RET_V_DIM:(h + 1) * RET_V_DIM] = (
                gj * jax.nn.sigmoid(gj) * (yc * lax.rsqrt(var + LN_EPS)))
        r_ref[h] = state


N_RWKV_IN = 12
N_RET_IN = 7


def _mixers_kernel(*refs, tb):
    rwkv_in = refs[:N_RWKV_IN]
    ret_in = refs[N_RWKV_IN:N_RWKV_IN + N_RET_IN]
    yr_ref, yt_ref, carry_ref, t_ref, r_ref = refs[N_RWKV_IN + N_RET_IN:]

    @pl.when(pl.program_id(1) == 0)
    def _():
        carry_ref[...] = jnp.zeros_like(carry_ref)
        t_ref[...] = jnp.zeros_like(t_ref)
        r_ref[...] = jnp.zeros_like(r_ref)

    _rwkv_body(*rwkv_in, yr_ref, carry_ref, t_ref, tb=tb)
    _ret_body(*ret_in, yt_ref, r_ref, tb=tb)


def _mixers(zr, mu, wwa, w0, a0, g_up, k_k, k_a, r_k, gn_g, gn_b, zt, pos, freq2, mask, xi_b, zeta_b, cd,
            *, tb=256):
    b, s, _ = zr.shape
    W = RWKV_WIDTH
    c = RET_CHUNK
    t = jnp.arange(tb)
    tril = ((t[:, None] // RWKV_CHUNK == t[None, :] // RWKV_CHUNK) & (t[None, :] <= t[:, None])).astype(BF16)
    tok = lambda w: pl.BlockSpec((1, tb, w), lambda i, j: (i, j, 0))
    vec = _const_spec((1, W))
    return pl.pallas_call(
        functools.partial(_mixers_kernel, tb=tb),
        grid=(b, s // tb),
        in_specs=[
            tok(RWKV_IN),
            _const_spec((tb, tb)),
            _const_spec((1, RWKV_IN)),
            _const_spec((LANES, 2 * W)),
            vec, vec,
            _const_spec((GATE_LORA, W)),
            vec, vec, vec, vec, vec,
            tok(RET_IN),
            pl.BlockSpec((tb, 1), lambda i, j, nb=s // tb: (i * nb + j, 0)),
            _const_spec((1, LANES)),
            _const_spec((RET_HEADS, c, c)),
            _const_spec((RET_HEADS, c, RET_QK_DIM)),
            _const_spec((RET_HEADS, c, RET_QK_DIM)),
            pl.BlockSpec(memory_space=pltpu.SMEM),
        ],
        out_specs=[tok(W), tok(RET_V_WIDTH)],
        out_shape=[jax.ShapeDtypeStruct((b, s, W), F32),
                   jax.ShapeDtypeStruct((b, s, RET_V_WIDTH), F32)],
        scratch_shapes=[
            pltpu.VMEM((1, RWKV_IN), F32),
            pltpu.VMEM((RWKV_PAIRS, LANES, LANES), F32),
            pltpu.VMEM((RET_HEADS, RET_QK_DIM, RET_V_DIM), F32),
        ],
        compiler_params=pltpu.CompilerParams(
            dimension_semantics=("arbitrary", "arbitrary"), vmem_limit_bytes=VMEM_LIMIT),
        name="mixers",
    )(zr, tril, mu, wwa, w0, a0, g_up, k_k, k_a, r_k, gn_g, gn_b, zt, pos, freq2, mask, xi_b, zeta_b, cd)


def _post_kernel(yr_ref, yt_ref, zg_ref, h_ref, p_ref, wbr_ref, wbt_ref, wo_ref, g2_ref, b2_ref,
                 win_ref, wout_ref, g3_ref, b3_ref, wg_ref, wp_ref, o_ref, acc_ref, *, fc):
    a = _dot(yr_ref[...], wbr_ref[...])
    b = _dot(yt_ref[...], wbt_ref[...])
    m = zg_ref[:, :D_MODEL] * a + zg_ref[:, D_MODEL:] * b
    h = _layer_norm(DN_ALPHA * h_ref[...] + _dot(m, wo_ref[...]), g2_ref[...], b2_ref[...])
    h = _ffn_block(h, win_ref, wout_ref, g3_ref, b3_ref, acc_ref, fc)
    gate = jax.nn.sigmoid(_dot(h, wg_ref[...]))
    o_ref[...] = h + gate * _dot(p_ref[...], wp_ref[...])


def _post(yr, yt, zg, h, p, wbr, wbt, wo, g2, b2, w_in, w_out, g3, b3, wg, wp, *, tm=512, fc=256):
    n = h.shape[0]
    tok = lambda w: pl.BlockSpec((tm, w), lambda i: (i, 0))
    vec = _const_spec((1, D_MODEL))
    return pl.pallas_call(
        functools.partial(_post_kernel, fc=fc),
        grid=(n // tm,),
        in_specs=[
            tok(RWKV_WIDTH), tok(RET_V_WIDTH), tok(GATE_IN), tok(D_MODEL), tok(PLE_DIM),
            _const_spec((RWKV_WIDTH, D_MODEL)),
            _const_spec((RET_V_WIDTH, D_MODEL)),
            _const_spec((D_MODEL, D_MODEL)),
            vec, vec,
            _const_spec((D_MODEL, 2 * D_FF)),
            _const_spec((D_FF, D_MODEL)),
            vec, vec,
            _const_spec((D_MODEL, D_MODEL)),
            _const_spec((PLE_DIM, D_MODEL)),
        ],
        out_specs=tok(D_MODEL),
        out_shape=jax.ShapeDtypeStruct((n, D_MODEL), F32),
        scratch_shapes=[pltpu.VMEM((tm, D_MODEL), F32)],
        compiler_params=pltpu.CompilerParams(
            dimension_semantics=("parallel",), vmem_limit_bytes=VMEM_LIMIT),
        name="post",
    )(yr, yt, zg, h, p, wbr, wbt, wo, g2, b2, w_in, w_out, g3, b3, wg, wp)


def _retention_constants():
    h, c = RET_HEADS, RET_CHUNK
    log_gamma = jnp.log(1.0 - jnp.exp2(-5.0 - jnp.arange(h, dtype=F32)))
    idx = jnp.arange(c, dtype=F32)
    rel = idx[:, None] - idx[None, :]
    mask = jnp.where(rel >= 0, jnp.exp(log_gamma[:, None, None] * jnp.maximum(rel, 0.0)), 0.0)
    xi = jnp.exp(log_gamma[:, None] * (idx + 1.0))
    zeta = jnp.exp(log_gamma[:, None] * (c - 1.0 - idx))
    cd = jnp.exp(log_gamma * c)
    xi_b = jnp.broadcast_to(xi[:, :, None], (h, c, RET_QK_DIM))
    zeta_b = jnp.broadcast_to(zeta[:, :, None], (h, c, RET_QK_DIM))
    half = RET_QK_DIM // 2
    inv_freq = ROPE_BASE ** (-jnp.arange(half, dtype=F32) / half)
    freq2 = jnp.concatenate([inv_freq, inv_freq])[None, :]
    return freq2, mask, xi_b, zeta_b, cd


def kernel(x, p, positions, ln1_g, ln1_b, ffn1_w_in, ffn1_w_out, w_mix_in, rwkv_mu, rwkv_w0, rwkv_w_up,
           rwkv_a0, rwkv_a_up, rwkv_g_up, rwkv_k_k, rwkv_k_a, rwkv_r_k, rwkv_gn_g, rwkv_gn_b,
           w_branch_rwkv, w_branch_ret, w_mix_out, ln2_g, ln2_b, ffn2_w_in, ffn2_w_out, ln3_g, ln3_b,
           ple_w_proj, ple_w_gate):
    b, s, d = x.shape
    n = b * s
    bf = lambda w: w.astype(BF16)
    freq2, mask, xi_b, zeta_b, cd = _retention_constants()
    pos = positions.astype(F32).reshape(n, 1)
    h = x.reshape(n, d)
    for i in range(DEPTH):
        h = _ffn(h, bf(ffn1_w_in[i]), bf(ffn1_w_out[i]), ln1_g[i][None], ln1_b[i][None])
        zr, zt, zg = _mix_in(h, bf(w_mix_in[i]))
        zeros = jnp.zeros((DECAY_LORA, RWKV_WIDTH), F32)
        wwa = jnp.concatenate([
            jnp.concatenate([rwkv_w_up[i], zeros], axis=1),
            jnp.concatenate([zeros, rwkv_a_up[i]], axis=1)], axis=0)
        yr, yt = _mixers(zr.reshape(b, s, RWKV_IN), rwkv_mu[i][None], bf(wwa), rwkv_w0[i][None],
                         rwkv_a0[i][None], bf(rwkv_g_up[i]), rwkv_k_k[i][None], rwkv_k_a[i][None],
                         rwkv_r_k[i].reshape(1, RWKV_WIDTH), rwkv_gn_g[i][None], rwkv_gn_b[i][None],
                         zt.reshape(b, s, RET_IN), pos, freq2, mask, xi_b, zeta_b, cd)
        h = _post(yr.reshape(n, RWKV_WIDTH), yt.reshape(n, RET_V_WIDTH), zg, h, p[i].reshape(n, PLE_DIM),
                  bf(w_branch_rwkv[i]), bf(w_branch_ret[i]), bf(w_mix_out[i]),
                  ln2_g[i][None], ln2_b[i][None],
                  bf(ffn2_w_in[i]), bf(ffn2_w_out[i]), ln3_g[i][None], ln3_b[i][None],
                  bf(ple_w_gate[i]), bf(ple_w_proj[i]))
    return h.reshape(b, s, d)
```

```python
import functools
import itertools
import math

import jax
import jax.numpy as jnp
from jax import lax
from jax.experimental import pallas as pl
from jax.experimental.pallas import tpu as pltpu

F32 = jnp.float32
BF16 = jnp.bfloat16

D_MODEL = 1024
PLE_DIM = 256
D_FF = 2816
RWKV_HEADS = 8
RWKV_HEAD_DIM = 64
RWKV_WIDTH = RWKV_HEADS * RWKV_HEAD_DIM
DECAY_LORA = 64
AAA_LORA = 64
GATE_LORA = 128
RWKV_GN_EPS = 64e-5
DECAY_SCALE = math.exp(-0.5)
RET_HEADS = 4
RET_QK_DIM = 128
RET_V_DIM = 256
RET_QK_WIDTH = RET_HEADS * RET_QK_DIM
RET_V_WIDTH = RET_HEADS * RET_V_DIM
RET_CHUNK = 128
ROPE_BASE = 10000.0
RWKV_IN = 3 * RWKV_WIDTH + DECAY_LORA + AAA_LORA + GATE_LORA
RET_IN = 2 * RET_QK_WIDTH + 2 * RET_V_WIDTH
GATE_IN = 2 * D_MODEL
DEPTH = 1
DN_ALPHA = (2 * DEPTH) ** 0.25
LN_EPS = 1e-5

LANES = 128
RWKV_CHUNK = 64
RWKV_PAIRS = RWKV_WIDTH // LANES
VMEM_LIMIT = 56 * 1024 * 1024
MIX_BLOCK = 256

RW16_COLS = 6 * RWKV_WIDTH
RW32_COLS = 4 * RWKV_WIDTH
RT16_COLS = 4 * RET_QK_WIDTH + RET_V_WIDTH


def _dot(a, b):
    return jnp.dot(a.astype(BF16), b.astype(BF16), preferred_element_type=F32)


def _dot_nt(a, b):
    return lax.dot_general(a.astype(BF16), b.astype(BF16), (((1,), (1,)), ((), ())),
                           preferred_element_type=F32)


def _dot_tn(a, b):
    return lax.dot_general(a.astype(BF16), b.astype(BF16), (((0,), (0,)), ((), ())),
                           preferred_element_type=F32)


def _bdot(a, b):
    return lax.dot_general(a.astype(BF16), b.astype(BF16), (((2,), (1,)), ((0,), (0,))),
                           preferred_element_type=F32)


def _bdot_nt(a, b):
    return lax.dot_general(a.astype(BF16), b.astype(BF16), (((2,), (2,)), ((0,), (0,))),
                           preferred_element_type=F32)


def _bdot_tn(a, b):
    return lax.dot_general(a.astype(BF16), b.astype(BF16), (((1,), (1,)), ((0,), (0,))),
                           preferred_element_type=F32)


def _dot_01(x, m01, *, left=False):
    hi = x.astype(BF16)
    lo = (x - hi.astype(F32)).astype(BF16)
    if left:
        dot = lambda t: jnp.dot(m01, t, preferred_element_type=F32)
    else:
        dot = lambda t: jnp.dot(t, m01, preferred_element_type=F32)
    return dot(hi) + dot(lo)


def _same_head_mask():
    jr = lax.broadcasted_iota(jnp.int32, (LANES, LANES), 0)
    jc = lax.broadcasted_iota(jnp.int32, (LANES, LANES), 1)
    return (jr < RWKV_HEAD_DIM) == (jc < RWKV_HEAD_DIM), jr == jc


def _head_sum(x, ones_bd):
    return jnp.concatenate(
        [_dot_01(x[:, p * LANES:(p + 1) * LANES], ones_bd) for p in range(RWKV_PAIRS)], axis=1)


def _layer_norm(y, g, b):
    mu = jnp.mean(y, -1, keepdims=True)
    yc = y - mu
    var = jnp.mean(yc * yc, -1, keepdims=True)
    return yc * lax.rsqrt(var + LN_EPS) * g + b


def _const_spec(shape):
    return pl.BlockSpec(shape, lambda *_: (0,) * len(shape), pipeline_mode=pl.Buffered(1))


def _ffn_block(x, win_ref, wout_ref, g_ref, b_ref, acc_ref, fc):
    xb = x.astype(BF16)
    for j in range(D_FF // fc):
        gate = jnp.dot(xb, win_ref[:, j * fc:(j + 1) * fc], preferred_element_type=F32)
        up = jnp.dot(xb, win_ref[:, D_FF + j * fc:D_FF + (j + 1) * fc], preferred_element_type=F32)
        act = (gate * jax.nn.sigmoid(gate) * up).astype(BF16)
        part = jnp.dot(act, wout_ref[j * fc:(j + 1) * fc, :], preferred_element_type=F32)
        if j == 0:
            acc_ref[...] = part
        else:
            acc_ref[...] += part
    y = DN_ALPHA * x + 0.5 * acc_ref[...]
    return _layer_norm(y, g_ref[...], b_ref[...])


def _ffn_kernel(x_ref, win_ref, wout_ref, g_ref, b_ref, o_ref, acc_ref, *, fc):
    o_ref[...] = _ffn_block(x_ref[...], win_ref, wout_ref, g_ref, b_ref, acc_ref, fc)


def _ffn(x, w_in, w_out, g, b, *, tm=512, fc=256):
    n = x.shape[0]
    return pl.pallas_call(
        functools.partial(_ffn_kernel, fc=fc),
        grid=(n // tm,),
        in_specs=[
            pl.BlockSpec((tm, D_MODEL), lambda i: (i, 0)),
            _const_spec((D_MODEL, 2 * D_FF)),
            _const_spec((D_FF, D_MODEL)),
            _const_spec((1, D_MODEL)),
            _const_spec((1, D_MODEL)),
        ],
        out_specs=pl.BlockSpec((tm, D_MODEL), lambda i: (i, 0)),
        out_shape=jax.ShapeDtypeStruct((n, D_MODEL), F32),
        scratch_shapes=[pltpu.VMEM((tm, D_MODEL), F32)],
        compiler_params=pltpu.CompilerParams(
            dimension_semantics=("parallel",), vmem_limit_bytes=VMEM_LIMIT),
        name="ffn",
    )(x, w_in, w_out, g, b)


def _mixin_kernel(h_ref, w_ref, tril_ref, mu_ref, wwa_ref, w0_ref, a0_ref, gup_ref, kk_ref, ka_ref, rk_ref,
                  pos_ref, freq_ref, xi_ref, zeta_ref,
                  rw16_ref, rw32_ref, rt16_ref, rtg_ref, zg_ref, carry_ref, *, tm, tiles_per_seq):
    L = RWKV_CHUNK
    W = RWKV_WIDTH
    nc = tm // L

    @pl.when(pl.program_id(0) % tiles_per_seq == 0)
    def _():
        carry_ref[...] = jnp.zeros_like(carry_ref)

    hb = h_ref[...].astype(BF16)

    def proj(lo, width):
        return jnp.dot(hb, w_ref[:, lo:lo + width], preferred_element_type=F32)

    z = proj(0, RWKV_IN)
    row = lax.broadcasted_iota(jnp.int32, z.shape, 0)
    prev = jnp.where(row == 0, carry_ref[...], pltpu.roll(z, 1, axis=0))
    carry_ref[...] = z[tm - 1:tm, :]
    zs = z + (prev - z) * mu_ref[...]

    r = zs[:, :W]
    k = zs[:, W:2 * W]
    v = zs[:, 2 * W:3 * W]
    dwa = zs[:, 3 * W:3 * W + LANES]
    dg = zs[:, 3 * W + LANES:]
    lane = lax.broadcasted_iota(jnp.int32, (tm, LANES), 1)
    dwa = jnp.where(lane < RWKV_HEAD_DIM, jnp.tanh(dwa), dwa)

    zqk = proj(RWKV_IN, 2 * RET_QK_WIDTH)

    wa = _dot(dwa, wwa_ref[...])
    g = _dot(jax.nn.sigmoid(dg), gup_ref[...])
    lw = -DECAY_SCALE * jax.nn.sigmoid(w0_ref[...] + wa[:, :W])
    a = jax.nn.sigmoid(a0_ref[...] + wa[:, W:])

    zv = proj(RWKV_IN + 2 * RET_QK_WIDTH, RET_V_WIDTH)

    cs = _dot_01(lw, tril_ref[...], left=True)

    rtg_ref[...] = proj(RWKV_IN + 2 * RET_QK_WIDTH + RET_V_WIDTH, RET_V_WIDTH)

    same_head, _ = _same_head_mask()
    ones_bd = jnp.where(same_head, 1.0, 0.0).astype(BF16)
    kk = k * kk_ref[...]
    kk = kk * lax.rsqrt(jnp.maximum(_head_sum(kk * kk, ones_bd), 1e-24))
    k2 = k * (1.0 + (a - 1.0) * ka_ref[...])
    bonus = _head_sum(r * k2 * rk_ref[...], ones_bd) * v

    zg_ref[...] = jax.nn.sigmoid(proj(RWKV_IN + RET_IN, GATE_IN))

    gam_l = jnp.concatenate(
        [jnp.broadcast_to(jnp.exp(cs[c * L + L - 1:c * L + L, :]), (L, W)) for c in range(nc)], axis=0)
    ka = kk * a
    inv = jnp.exp(-cs)
    end = gam_l * inv
    rw16_ref[:, 0 * W:1 * W] = (kk * jnp.exp(cs - lw)).astype(BF16)
    rw16_ref[:, 1 * W:2 * W] = (k2 * inv).astype(BF16)
    rw16_ref[:, 2 * W:3 * W] = (ka * inv).astype(BF16)
    rw16_ref[:, 3 * W:4 * W] = (k2 * end).astype(BF16)
    rw16_ref[:, 4 * W:5 * W] = (ka * end).astype(BF16)
    rw16_ref[:, 5 * W:6 * W] = v.astype(BF16)
    rw32_ref[:, 0 * W:1 * W] = r * jnp.exp(cs)
    rw32_ref[:, 1 * W:2 * W] = bonus
    rw32_ref[:, 2 * W:3 * W] = g
    rw32_ref[:, 3 * W:4 * W] = gam_l

    ang = pos_ref[...] * freq_ref[...]
    cos2 = jnp.cos(ang)
    sin2 = jnp.sin(ang)
    sin2 = jnp.where(lane < RET_QK_DIM // 2, -sin2, sin2)

    def rot(x):
        return x * cos2 + pltpu.roll(x, RET_QK_DIM // 2, axis=1) * sin2

    for h in range(RET_HEADS):
        cols = slice(h * RET_QK_DIM, (h + 1) * RET_QK_DIM)
        q = rot(zqk[:, cols])
        kr = rot(zqk[:, RET_QK_WIDTH + h * RET_QK_DIM:RET_QK_WIDTH + (h + 1) * RET_QK_DIM])
        kr = kr * (RET_QK_DIM ** -0.5)
        for part, val in enumerate((q, kr, kr * zeta_ref[h], q * xi_ref[h])):
            c0 = part * RET_QK_WIDTH + h * RET_QK_DIM
            rt16_ref[:, c0:c0 + RET_QK_DIM] = val.astype(BF16)
    rt16_ref[:, 4 * RET_QK_WIDTH:] = zv.astype(BF16)


def _mix_in(h, w, mu, wwa, w0, a0, g_up, k_k, k_a, r_k, pos, freq2, xi_t, zeta_t, *, seq, tm=MIX_BLOCK):
    n = h.shape[0]
    W = RWKV_WIDTH
    t = jnp.arange(tm)
    tril = ((t[:, None] // RWKV_CHUNK == t[None, :] // RWKV_CHUNK) & (t[None, :] <= t[:, None])).astype(BF16)
    tok = lambda width: pl.BlockSpec((tm, width), lambda i: (i, 0))
    vec = _const_spec((1, W))
    widths = (RW16_COLS, RW32_COLS, RT16_COLS, RET_V_WIDTH, GATE_IN)
    dtypes = (BF16, F32, BF16, F32, F32)
    return pl.pallas_call(
        functools.partial(_mixin_kernel, tm=tm, tiles_per_seq=seq // tm),
        grid=(n // tm,),
        in_specs=[
            tok(D_MODEL),
            _const_spec((D_MODEL, RWKV_IN + RET_IN + GATE_IN)),
            _const_spec((tm, tm)),
            _const_spec((1, RWKV_IN)),
            _const_spec((LANES, 2 * W)),
            vec, vec,
            _const_spec((GATE_LORA, W)),
            vec, vec, vec,
            tok(1),
            _const_spec((1, LANES)),
            _const_spec((RET_HEADS, tm, RET_QK_DIM)),
            _const_spec((RET_HEADS, tm, RET_QK_DIM)),
        ],
        out_specs=[tok(width) for width in widths],
        out_shape=[jax.ShapeDtypeStruct((n, width), dt) for width, dt in zip(widths, dtypes)],
        scratch_shapes=[pltpu.VMEM((1, RWKV_IN), F32)],
        compiler_params=pltpu.CompilerParams(
            dimension_semantics=("arbitrary",), vmem_limit_bytes=VMEM_LIMIT),
        name="mix_in",
    )(h, w, tril, mu, wwa, w0, a0, g_up, k_k, k_a, r_k, pos, freq2, xi_t, zeta_t)


def _rwkv_body(rw16_ref, rw32_ref, gng_ref, gnb_ref, o_ref, t_ref, *, tb):
    L = RWKV_CHUNK
    W = RWKV_WIDTH
    nc = tb // L

    def tiles(ref, col0, rows=L):
        return jnp.stack([ref[0, c * L:c * L + rows, col0 + p * LANES:col0 + (p + 1) * LANES]
                          for c in range(nc) for p in range(RWKV_PAIRS)])

    aq = tiles(rw16_ref, 0 * W)
    kd = tiles(rw16_ref, 1 * W)
    bd = tiles(rw16_ref, 2 * W)
    ke = tiles(rw16_ref, 3 * W)
    be = tiles(rw16_ref, 4 * W)
    vv = tiles(rw16_ref, 5 * W)
    rq32 = tiles(rw32_ref, 0 * W)
    rq = rq32.astype(BF16)
    gam_l = tiles(rw32_ref, 3 * W, rows=1)

    same_head, diag = _same_head_mask()
    trow = lax.broadcasted_iota(jnp.int32, (L, LANES), 0)
    scol = lax.broadcasted_iota(jnp.int32, (L, LANES), 1)
    m_left = scol < RWKV_HEAD_DIM
    scol = scol & (L - 1)
    strict = scol < trow
    incl = scol <= trow
    eye_pair = jnp.where(scol == trow, 1.0, 0.0).astype(F32)

    def bd2(x):
        zero = jnp.zeros_like(x)
        return jnp.concatenate([jnp.where(m_left, x, zero), jnp.where(m_left, zero, x)], axis=1)

    m = _bdot_nt(jnp.concatenate([aq, rq], axis=1),
                 jnp.concatenate([bd2(kd), bd2(bd)], axis=1))
    mak = jnp.where(strict, m[:, :L, :LANES], 0.0).astype(BF16)
    mab = jnp.where(strict, m[:, :L, LANES:], 0.0)
    mrk = jnp.where(incl, m[:, L:, :LANES], 0.0).astype(BF16)
    mrb = jnp.where(incl, m[:, L:, LANES:], 0.0).astype(BF16)
    yield

    npow = mab.astype(BF16)
    pinv = eye_pair - mab
    for _ in range(5):
        npow = _bdot(npow, bd2(npow)).astype(BF16)
        pinv = pinv + _bdot(pinv, bd2(npow))
        yield

    makv = _bdot(mak, bd2(vv)).astype(BF16)
    au = _bdot(pinv, jnp.concatenate([bd2(aq), bd2(makv)], axis=2)).astype(BF16)
    ahat = au[:, :, :LANES]
    uhat = au[:, :, LANES:]
    yield
    qhat = rq32 - _bdot(mrb, bd2(ahat))
    yhat = _bdot(jnp.concatenate([mrk, -mrb], axis=2),
                 jnp.concatenate([bd2(vv), bd2(uhat)], axis=1))
    yield
    bta = _bdot_tn(be, au)
    ktv = _bdot_tn(ke, vv)
    gmat = jnp.where(same_head, jnp.where(diag, gam_l, 0.0) - bta[:, :, :LANES], 0.0)
    hmat = jnp.where(same_head, ktv - bta[:, :, LANES:], 0.0)
    yield

    state = t_ref[...]
    y_rows = []
    for c in range(nc):
        sel = slice(c * RWKV_PAIRS, (c + 1) * RWKV_PAIRS)
        yc = _bdot(qhat[sel], state) + yhat[sel]
        y_rows.append(jnp.concatenate([yc[p] for p in range(RWKV_PAIRS)], axis=1))
        state = _bdot(gmat[sel], state) + hmat[sel]
        yield
    t_ref[...] = state

    y = jnp.concatenate(y_rows, axis=0)
    ones_bd = jnp.where(same_head, 1.0, 0.0).astype(BF16)
    mean = _head_sum(y, ones_bd) * (1.0 / RWKV_HEAD_DIM)
    yc = y - mean
    var = _head_sum(yc * yc, ones_bd) * (1.0 / RWKV_HEAD_DIM)
    yn = yc * lax.rsqrt(var + RWKV_GN_EPS) * gng_ref[...] + gnb_ref[...]
    o_ref[0] = (yn + rw32_ref[0, :, 1 * W:2 * W]) * rw32_ref[0, :, 2 * W:3 * W]


def _ret_head(h, rt16_ref, rtg_ref, mask_ref, cd_ref, o_ref, r_ref, tb):
    c = RET_CHUNK
    nj = tb // c
    rows = [slice(j * c, (j + 1) * c) for j in range(nj)]

    def part(p, r):
        c0 = p * RET_QK_WIDTH + h * RET_QK_DIM
        return rt16_ref[0, r, c0:c0 + RET_QK_DIM]

    v0 = 4 * RET_QK_WIDTH + h * RET_V_DIM
    vb = [rt16_ref[0, r, v0:v0 + RET_V_DIM] for r in rows]
    scores = [_dot_nt(part(0, r), part(1, r)) for r in rows]
    ktv = [_dot_tn(part(2, rows[j]), vb[j]) for j in range(nj)]
    yield
    states = [r_ref[h]]
    for j in range(nj):
        states.append(states[j] * cd_ref[h] + ktv[j])
    r_ref[h] = states[nj]
    lhs = [jnp.concatenate([(scores[j] * mask_ref[h]).astype(BF16), part(3, rows[j])], axis=1)
           for j in range(nj)]
    rhs = [jnp.concatenate([vb[j], states[j].astype(BF16)], axis=0) for j in range(nj)]
    yield
    ys = [jnp.dot(lhs[j], rhs[j], preferred_element_type=F32) for j in range(nj)]
    yield
    for j in range(nj):
        y = ys[j]
        cols = slice(h * RET_V_DIM, (h + 1) * RET_V_DIM)
        g = rtg_ref[0, rows[j], cols]
        mu = jnp.mean(y, -1, keepdims=True)
        yc = y - mu
        var = jnp.mean(yc * yc, -1, keepdims=True)
        o_ref[0, rows[j], cols] = g * jax.nn.sigmoid(g) * (yc * lax.rsqrt(var + LN_EPS))
    yield


def _ret_body(rt16_ref, rtg_ref, mask_ref, cd_ref, o_ref, r_ref, *, tb):
    live = []
    pending = [_ret_head(h, rt16_ref, rtg_ref, mask_ref, cd_ref, o_ref, r_ref, tb) for h in range(RET_HEADS)]
    while live or pending:
        if pending:
            live.append(pending.pop(0))
        for gen in list(live):
            if next(gen, StopIteration) is StopIteration:
                live.remove(gen)
        yield


def _mixers_kernel(rw16_ref, rw32_ref, rt16_ref, rtg_ref, gng_ref, gnb_ref, mask_ref, cd_ref,
                   yr_ref, yt_ref, t_ref, r_ref, *, tb):
    @pl.when(pl.program_id(1) == 0)
    def _():
        t_ref[...] = jnp.zeros_like(t_ref)
        r_ref[...] = jnp.zeros_like(r_ref)

    rwkv = _rwkv_body(rw16_ref, rw32_ref, gng_ref, gnb_ref, yr_ref, t_ref, tb=tb)
    ret = _ret_body(rt16_ref, rtg_ref, mask_ref, cd_ref, yt_ref, r_ref, tb=tb)
    for _ in itertools.zip_longest(rwkv, ret):
        pass


def _mixers(rw16, rw32, rt16, rtg, gn_g, gn_b, mask, cd, *, tb=MIX_BLOCK):
    b, s, _ = rw16.shape
    c = RET_CHUNK
    tok = lambda w: pl.BlockSpec((1, tb, w), lambda i, j: (i, j, 0))
    vec = _const_spec((1, RWKV_WIDTH))
    return pl.pallas_call(
        functools.partial(_mixers_kernel, tb=tb),
        grid=(b, s // tb),
        in_specs=[
            tok(RW16_COLS), tok(RW32_COLS), tok(RT16_COLS), tok(RET_V_WIDTH),
            vec, vec,
            _const_spec((RET_HEADS, c, c)),
            pl.BlockSpec(memory_space=pltpu.SMEM),
        ],
        out_specs=[tok(RWKV_WIDTH), tok(RET_V_WIDTH)],
        out_shape=[jax.ShapeDtypeStruct((b, s, RWKV_WIDTH), F32),
                   jax.ShapeDtypeStruct((b, s, RET_V_WIDTH), F32)],
        scratch_shapes=[
            pltpu.VMEM((RWKV_PAIRS, LANES, LANES), F32),
            pltpu.VMEM((RET_HEADS, RET_QK_DIM, RET_V_DIM), F32),
        ],
        compiler_params=pltpu.CompilerParams(
            dimension_semantics=("arbitrary", "arbitrary"), vmem_limit_bytes=VMEM_LIMIT),
        name="mixers",
    )(rw16, rw32, rt16, rtg, gn_g, gn_b, mask, cd)


def _post_kernel(yr_ref, yt_ref, zg_ref, h_ref, p_ref, wbr_ref, wbt_ref, wo_ref, g2_ref, b2_ref,
                 win_ref, wout_ref, g3_ref, b3_ref, wg_ref, wp_ref, o_ref, acc_ref, *, fc):
    a = _dot(yr_ref[...], wbr_ref[...])
    b = _dot(yt_ref[...], wbt_ref[...])
    m = zg_ref[:, :D_MODEL] * a + zg_ref[:, D_MODEL:] * b
    h = _layer_norm(DN_ALPHA * h_ref[...] + _dot(m, wo_ref[...]), g2_ref[...], b2_ref[...])
    h = _ffn_block(h, win_ref, wout_ref, g3_ref, b3_ref, acc_ref, fc)
    gate = jax.nn.sigmoid(_dot(h, wg_ref[...]))
    o_ref[...] = h + gate * _dot(p_ref[...], wp_ref[...])


def _post(yr, yt, zg, h, p, wbr, wbt, wo, g2, b2, w_in, w_out, g3, b3, wg, wp, *, tm=512, fc=256):
    n = h.shape[0]
    tok = lambda w: pl.BlockSpec((tm, w), lambda i: (i, 0))
    vec = _const_spec((1, D_MODEL))
    return pl.pallas_call(
        functools.partial(_post_kernel, fc=fc),
        grid=(n // tm,),
        in_specs=[
            tok(RWKV_WIDTH), tok(RET_V_WIDTH), tok(GATE_IN), tok(D_MODEL), tok(PLE_DIM),
            _const_spec((RWKV_WIDTH, D_MODEL)),
            _const_spec((RET_V_WIDTH, D_MODEL)),
            _const_spec((D_MODEL, D_MODEL)),
            vec, vec,
            _const_spec((D_MODEL, 2 * D_FF)),
            _const_spec((D_FF, D_MODEL)),
            vec, vec,
            _const_spec((D_MODEL, D_MODEL)),
            _const_spec((PLE_DIM, D_MODEL)),
        ],
        out_specs=tok(D_MODEL),
        out_shape=jax.ShapeDtypeStruct((n, D_MODEL), F32),
        scratch_shapes=[pltpu.VMEM((tm, D_MODEL), F32)],
        compiler_params=pltpu.CompilerParams(
            dimension_semantics=("parallel",), vmem_limit_bytes=VMEM_LIMIT),
        name="post",
    )(yr, yt, zg, h, p, wbr, wbt, wo, g2, b2, w_in, w_out, g3, b3, wg, wp)


def _retention_constants(tm):
    h, c = RET_HEADS, RET_CHUNK
    log_gamma = jnp.log(1.0 - jnp.exp2(-5.0 - jnp.arange(h, dtype=F32)))
    idx = jnp.arange(c, dtype=F32)
    rel = idx[:, None] - idx[None, :]
    mask = jnp.where(rel >= 0, jnp.exp(log_gamma[:, None, None] * jnp.maximum(rel, 0.0)), 0.0)
    xi = jnp.exp(log_gamma[:, None] * (idx + 1.0))
    zeta = jnp.exp(log_gamma[:, None] * (c - 1.0 - idx))
    cd = jnp.exp(log_gamma * c)
    xi_t = jnp.broadcast_to(jnp.tile(xi, (1, tm // c))[:, :, None], (h, tm, RET_QK_DIM))
    zeta_t = jnp.broadcast_to(jnp.tile(zeta, (1, tm // c))[:, :, None], (h, tm, RET_QK_DIM))
    half = RET_QK_DIM // 2
    inv_freq = ROPE_BASE ** (-jnp.arange(half, dtype=F32) / half)
    freq2 = jnp.concatenate([inv_freq, inv_freq])[None, :]
    return freq2, mask, xi_t, zeta_t, cd


def kernel(x, p, positions, ln1_g, ln1_b, ffn1_w_in, ffn1_w_out, w_mix_in, rwkv_mu, rwkv_w0, rwkv_w_up,
           rwkv_a0, rwkv_a_up, rwkv_g_up, rwkv_k_k, rwkv_k_a, rwkv_r_k, rwkv_gn_g, rwkv_gn_b,
           w_branch_rwkv, w_branch_ret, w_mix_out, ln2_g, ln2_b, ffn2_w_in, ffn2_w_out, ln3_g, ln3_b,
           ple_w_proj, ple_w_gate):
    b, s, d = x.shape
    n = b * s
    bf = lambda w: w.astype(BF16)
    freq2, mask, xi_t, zeta_t, cd = _retention_constants(MIX_BLOCK)
    pos = positions.astype(F32).reshape(n, 1)
    h = x.reshape(n, d)
    for i in range(DEPTH):
        h = _ffn(h, bf(ffn1_w_in[i]), bf(ffn1_w_out[i]), ln1_g[i][None], ln1_b[i][None])
        zeros = jnp.zeros((DECAY_LORA, RWKV_WIDTH), F32)
        wwa = jnp.concatenate([
            jnp.concatenate([rwkv_w_up[i], zeros], axis=1),
            jnp.concatenate([zeros, rwkv_a_up[i]], axis=1)], axis=0)
        rw16, rw32, rt16, rtg, zg = _mix_in(
            h, bf(w_mix_in[i]), rwkv_mu[i][None], bf(wwa), rwkv_w0[i][None], rwkv_a0[i][None],
            bf(rwkv_g_up[i]), rwkv_k_k[i][None], rwkv_k_a[i][None], rwkv_r_k[i].reshape(1, RWKV_WIDTH),
            pos, freq2, xi_t, zeta_t, seq=s)
        yr, yt = _mixers(rw16.reshape(b, s, RW16_COLS), rw32.reshape(b, s, RW32_COLS),
                         rt16.reshape(b, s, RT16_COLS), rtg.reshape(b, s, RET_V_WIDTH),
                         rwkv_gn_g[i][None], rwkv_gn_b[i][None], mask, cd)
        h = _post(yr.reshape(n, RWKV_WIDTH), yt.reshape(n, RET_V_WIDTH), zg, h, p[i].reshape(n, PLE_DIM),
                  bf(w_branch_rwkv[i]), bf(w_branch_ret[i]), bf(w_mix_out[i]),
                  ln2_g[i][None], ln2_b[i][None],
                  bf(ffn2_w_in[i]), bf(ffn2_w_out[i]), ln3_g[i][None], ln3_b[i][None],
                  bf(ple_w_gate[i]), bf(ple_w_proj[i]))
    return h.reshape(b, s, d)
```

```python
import functools
import itertools
import math

import jax
import jax.numpy as jnp
from jax import lax
from jax.experimental import pallas as pl
from jax.experimental.pallas import tpu as pltpu

F32 = jnp.float32
BF16 = jnp.bfloat16

D_MODEL = 1024
PLE_DIM = 256
D_FF = 2816
RWKV_HEADS = 8
RWKV_HEAD_DIM = 64
RWKV_WIDTH = RWKV_HEADS * RWKV_HEAD_DIM
DECAY_LORA = 64
AAA_LORA = 64
GATE_LORA = 128
RWKV_GN_EPS = 64e-5
DECAY_SCALE = math.exp(-0.5)
RET_HEADS = 4
RET_QK_DIM = 128
RET_V_DIM = 256
RET_QK_WIDTH = RET_HEADS * RET_QK_DIM
RET_V_WIDTH = RET_HEADS * RET_V_DIM
RET_CHUNK = 128
ROPE_BASE = 10000.0
RWKV_IN = 3 * RWKV_WIDTH + DECAY_LORA + AAA_LORA + GATE_LORA
RET_IN = 2 * RET_QK_WIDTH + 2 * RET_V_WIDTH
GATE_IN = 2 * D_MODEL
DEPTH = 1
DN_ALPHA = (2 * DEPTH) ** 0.25
LN_EPS = 1e-5

LANES = 128
RWKV_CHUNK = 64
RWKV_PAIRS = RWKV_WIDTH // LANES
MXU_WIDTH = 256
RWKV_GROUP = MXU_WIDTH
RWKV_GROUPS = RWKV_WIDTH // RWKV_GROUP
VMEM_LIMIT = 56 * 1024 * 1024
MIX_BLOCK = 256

RW16_COLS = 6 * RWKV_WIDTH
RW32_COLS = 4 * RWKV_WIDTH
RT16_COLS = 4 * RET_QK_WIDTH + RET_V_WIDTH


def _dot(a, b):
    return jnp.dot(a.astype(BF16), b.astype(BF16), preferred_element_type=F32)


def _dot_nt(a, b):
    return lax.dot_general(a.astype(BF16), b.astype(BF16), (((1,), (1,)), ((), ())),
                           preferred_element_type=F32)


def _dot_tn(a, b):
    return lax.dot_general(a.astype(BF16), b.astype(BF16), (((0,), (0,)), ((), ())),
                           preferred_element_type=F32)


def _bdot(a, b):
    return lax.dot_general(a.astype(BF16), b.astype(BF16), (((2,), (1,)), ((0,), (0,))),
                           preferred_element_type=F32)


def _bdot_nt(a, b):
    return lax.dot_general(a.astype(BF16), b.astype(BF16), (((2,), (2,)), ((0,), (0,))),
                           preferred_element_type=F32)


def _bdot_tn(a, b):
    return lax.dot_general(a.astype(BF16), b.astype(BF16), (((1,), (1,)), ((0,), (0,))),
                           preferred_element_type=F32)


def _dot_01(x, m01, *, left=False):
    hi = x.astype(BF16)
    lo = (x - hi.astype(F32)).astype(BF16)
    if left:
        dot = lambda t: jnp.dot(m01, t, preferred_element_type=F32)
    else:
        dot = lambda t: jnp.dot(t, m01, preferred_element_type=F32)
    return dot(hi) + dot(lo)


HEAD_SHIFT = RWKV_HEAD_DIM.bit_length() - 1


def _same_head_mask(width):
    jr = lax.broadcasted_iota(jnp.int32, (width, width), 0)
    jc = lax.broadcasted_iota(jnp.int32, (width, width), 1)
    return (jr >> HEAD_SHIFT) == (jc >> HEAD_SHIFT), jr == jc


def _head_sum(x):
    same_head, _ = _same_head_mask(RWKV_GROUP)
    ones_bd = jnp.where(same_head, 1.0, 0.0).astype(BF16)
    return jnp.concatenate(
        [jnp.dot(x[:, q * RWKV_GROUP:(q + 1) * RWKV_GROUP].astype(BF16), ones_bd, preferred_element_type=F32)
         for q in range(RWKV_GROUPS)], axis=1)


def _layer_norm(y, g, b):
    mu = jnp.mean(y, -1, keepdims=True)
    yc = y - mu
    var = jnp.mean(yc * yc, -1, keepdims=True)
    return yc * lax.rsqrt(var + LN_EPS) * g + b


def _const_spec(shape):
    return pl.BlockSpec(shape, lambda *_: (0,) * len(shape), pipeline_mode=pl.Buffered(1))


def _ffn_block(x, win_ref, wout_ref, g_ref, b_ref, acc_ref, fc):
    xb = x.astype(BF16)
    for j in range(D_FF // fc):
        gate = jnp.dot(xb, win_ref[:, j * fc:(j + 1) * fc], preferred_element_type=F32)
        up = jnp.dot(xb, win_ref[:, D_FF + j * fc:D_FF + (j + 1) * fc], preferred_element_type=F32)
        act = (gate * jax.nn.sigmoid(gate) * up).astype(BF16)
        part = jnp.dot(act, wout_ref[j * fc:(j + 1) * fc, :], preferred_element_type=F32)
        if j == 0:
            acc_ref[...] = part
        else:
            acc_ref[...] += part
    y = DN_ALPHA * x + 0.5 * acc_ref[...]
    return _layer_norm(y, g_ref[...], b_ref[...])


def _ffn_kernel(x_ref, win_ref, wout_ref, g_ref, b_ref, o_ref, acc_ref, *, fc):
    o_ref[...] = _ffn_block(x_ref[...], win_ref, wout_ref, g_ref, b_ref, acc_ref, fc)


def _ffn(x, w_in, w_out, g, b, *, tm=512, fc=256):
    n = x.shape[0]
    return pl.pallas_call(
        functools.partial(_ffn_kernel, fc=fc),
        grid=(n // tm,),
        in_specs=[
            pl.BlockSpec((tm, D_MODEL), lambda i: (i, 0)),
            _const_spec((D_MODEL, 2 * D_FF)),
            _const_spec((D_FF, D_MODEL)),
            _const_spec((1, D_MODEL)),
            _const_spec((1, D_MODEL)),
        ],
        out_specs=pl.BlockSpec((tm, D_MODEL), lambda i: (i, 0)),
        out_shape=jax.ShapeDtypeStruct((n, D_MODEL), F32),
        scratch_shapes=[pltpu.VMEM((tm, D_MODEL), F32)],
        compiler_params=pltpu.CompilerParams(
            dimension_semantics=("parallel",), vmem_limit_bytes=VMEM_LIMIT),
        name="ffn",
    )(x, w_in, w_out, g, b)


def _mixin_kernel(h_ref, w_ref, tril_ref, mu_ref, wwa_ref, w0_ref, a0_ref, gup_ref, kk_ref, ka_ref, rk_ref,
                  pos_ref, freq_ref, xi_ref, zeta_ref,
                  rw16_ref, rw32_ref, rt16_ref, rtg_ref, zg_ref, carry_ref, *, tm, tiles_per_seq):
    L = RWKV_CHUNK
    W = RWKV_WIDTH
    nc = tm // L

    @pl.when(pl.program_id(0) % tiles_per_seq == 0)
    def _():
        carry_ref[...] = jnp.zeros_like(carry_ref)

    hb = h_ref[...].astype(BF16)

    def proj(lo, width):
        return jnp.dot(hb, w_ref[:, lo:lo + width], preferred_element_type=F32)

    z = proj(0, RWKV_IN)
    row = lax.broadcasted_iota(jnp.int32, z.shape, 0)
    prev = jnp.where(row == 0, carry_ref[...], pltpu.roll(z, 1, axis=0))
    carry_ref[...] = z[tm - 1:tm, :]
    zs = z + (prev - z) * mu_ref[...]

    r = zs[:, :W]
    k = zs[:, W:2 * W]
    v = zs[:, 2 * W:3 * W]
    dwa = zs[:, 3 * W:3 * W + LANES]
    dg = zs[:, 3 * W + LANES:]
    lane = lax.broadcasted_iota(jnp.int32, (tm, LANES), 1)
    dwa = jnp.where(lane < RWKV_HEAD_DIM, jnp.tanh(dwa), dwa)

    zqk = proj(RWKV_IN, 2 * RET_QK_WIDTH)

    wa = _dot(dwa, wwa_ref[...])
    g = _dot(jax.nn.sigmoid(dg), gup_ref[...])
    lw = -DECAY_SCALE * jax.nn.sigmoid(w0_ref[...] + wa[:, :W])
    a = jax.nn.sigmoid(a0_ref[...] + wa[:, W:])

    zg_ref[:, :D_MODEL] = jax.nn.sigmoid(proj(RWKV_IN + RET_IN, D_MODEL))

    cs = _dot_01(lw, tril_ref[...], left=True)

    zg_ref[:, D_MODEL:] = jax.nn.sigmoid(proj(RWKV_IN + RET_IN + D_MODEL, D_MODEL))

    kk = k * kk_ref[...]
    kk = kk * lax.rsqrt(jnp.maximum(_head_sum(kk * kk), 1e-24))
    k2 = k * (1.0 + (a - 1.0) * ka_ref[...])
    bonus = _head_sum(r * k2 * rk_ref[...]) * v

    zv = proj(RWKV_IN + 2 * RET_QK_WIDTH, RET_V_WIDTH)
    rtg_ref[...] = proj(RWKV_IN + 2 * RET_QK_WIDTH + RET_V_WIDTH, RET_V_WIDTH)

    gam_l = jnp.concatenate(
        [jnp.broadcast_to(jnp.exp(cs[c * L + L - 1:c * L + L, :]), (L, W)) for c in range(nc)], axis=0)
    ka = kk * a
    inv = jnp.exp(-cs)
    end = gam_l * inv
    rw16_ref[:, 0 * W:1 * W] = (kk * jnp.exp(cs - lw)).astype(BF16)
    rw16_ref[:, 1 * W:2 * W] = (k2 * inv).astype(BF16)
    rw16_ref[:, 2 * W:3 * W] = (ka * inv).astype(BF16)
    rw16_ref[:, 3 * W:4 * W] = (k2 * end).astype(BF16)
    rw16_ref[:, 4 * W:5 * W] = (ka * end).astype(BF16)
    rw16_ref[:, 5 * W:6 * W] = v.astype(BF16)
    rw32_ref[:, 0 * W:1 * W] = r * jnp.exp(cs)
    rw32_ref[:, 1 * W:2 * W] = bonus
    rw32_ref[:, 2 * W:3 * W] = g
    rw32_ref[:, 3 * W:4 * W] = gam_l

    ang = pos_ref[...] * freq_ref[...]
    cos2 = jnp.cos(ang)
    sin2 = jnp.sin(ang)
    sin2 = jnp.where(lane < RET_QK_DIM // 2, -sin2, sin2)

    def rot(x):
        return x * cos2 + pltpu.roll(x, RET_QK_DIM // 2, axis=1) * sin2

    for h in range(RET_HEADS):
        cols = slice(h * RET_QK_DIM, (h + 1) * RET_QK_DIM)
        q = rot(zqk[:, cols])
        kr = rot(zqk[:, RET_QK_WIDTH + h * RET_QK_DIM:RET_QK_WIDTH + (h + 1) * RET_QK_DIM])
        kr = kr * (RET_QK_DIM ** -0.5)
        for part, val in enumerate((q, kr, kr * zeta_ref[h], q * xi_ref[h])):
            c0 = part * RET_QK_WIDTH + h * RET_QK_DIM
            rt16_ref[:, c0:c0 + RET_QK_DIM] = val.astype(BF16)
    rt16_ref[:, 4 * RET_QK_WIDTH:] = zv.astype(BF16)


def _mix_in(h, w, mu, wwa, w0, a0, g_up, k_k, k_a, r_k, pos, freq2, xi_t, zeta_t, *, seq, tm=MIX_BLOCK):
    n = h.shape[0]
    W = RWKV_WIDTH
    t = jnp.arange(tm)
    tril = ((t[:, None] // RWKV_CHUNK == t[None, :] // RWKV_CHUNK) & (t[None, :] <= t[:, None])).astype(BF16)
    tok = lambda width: pl.BlockSpec((tm, width), lambda i: (i, 0))
    vec = _const_spec((1, W))
    widths = (RW16_COLS, RW32_COLS, RT16_COLS, RET_V_WIDTH, GATE_IN)
    dtypes = (BF16, F32, BF16, F32, F32)
    return pl.pallas_call(
        functools.partial(_mixin_kernel, tm=tm, tiles_per_seq=seq // tm),
        grid=(n // tm,),
        in_specs=[
            tok(D_MODEL),
            _const_spec((D_MODEL, RWKV_IN + RET_IN + GATE_IN)),
            _const_spec((tm, tm)),
            _const_spec((1, RWKV_IN)),
            _const_spec((LANES, 2 * W)),
            vec, vec,
            _const_spec((GATE_LORA, W)),
            vec, vec, vec,
            tok(1),
            _const_spec((1, LANES)),
            _const_spec((RET_HEADS, tm, RET_QK_DIM)),
            _const_spec((RET_HEADS, tm, RET_QK_DIM)),
        ],
        out_specs=[tok(width) for width in widths],
        out_shape=[jax.ShapeDtypeStruct((n, width), dt) for width, dt in zip(widths, dtypes)],
        scratch_shapes=[pltpu.VMEM((1, RWKV_IN), F32)],
        compiler_params=pltpu.CompilerParams(
            dimension_semantics=("arbitrary",), vmem_limit_bytes=VMEM_LIMIT),
        name="mix_in",
    )(h, w, tril, mu, wwa, w0, a0, g_up, k_k, k_a, r_k, pos, freq2, xi_t, zeta_t)


def _rwkv_body(rw16_ref, rw32_ref, gng_ref, gnb_ref, o_ref, t_ref, *, tb):
    L = RWKV_CHUNK
    W = RWKV_WIDTH
    nc = tb // L

    def tiles(ref, col0, rows=L):
        return jnp.stack([ref[0, c * L:c * L + rows, col0 + p * LANES:col0 + (p + 1) * LANES]
                          for c in range(nc) for p in range(RWKV_PAIRS)])

    aq = tiles(rw16_ref, 0 * W)
    kd = tiles(rw16_ref, 1 * W)
    bd = tiles(rw16_ref, 2 * W)
    ke = tiles(rw16_ref, 3 * W)
    be = tiles(rw16_ref, 4 * W)
    vv = tiles(rw16_ref, 5 * W)
    rq32 = tiles(rw32_ref, 0 * W)
    rq = rq32.astype(BF16)
    gam_l = tiles(rw32_ref, 3 * W, rows=1)

    same_head, diag = _same_head_mask(LANES)
    trow = lax.broadcasted_iota(jnp.int32, (L, LANES), 0)
    scol = lax.broadcasted_iota(jnp.int32, (L, LANES), 1)
    m_left = scol < RWKV_HEAD_DIM
    scol = scol & (L - 1)
    strict = scol < trow
    incl = scol <= trow
    eye_pair = jnp.where(scol == trow, 1.0, 0.0).astype(F32)

    def bd2(x):
        zero = jnp.zeros_like(x)
        return jnp.concatenate([jnp.where(m_left, x, zero), jnp.where(m_left, zero, x)], axis=1)

    m = _bdot_nt(jnp.concatenate([aq, rq], axis=1),
                 jnp.concatenate([bd2(kd), bd2(bd)], axis=1))
    mak = jnp.where(strict, m[:, :L, :LANES], 0.0).astype(BF16)
    mab = jnp.where(strict, m[:, :L, LANES:], 0.0)
    mrk = jnp.where(incl, m[:, L:, :LANES], 0.0).astype(BF16)
    mrb = jnp.where(incl, m[:, L:, LANES:], 0.0).astype(BF16)
    yield

    npow = mab.astype(BF16)
    pinv = eye_pair - mab
    for _ in range(5):
        npow = _bdot(npow, bd2(npow)).astype(BF16)
        pinv = pinv + _bdot(pinv, bd2(npow))
        yield

    makv = _bdot(mak, bd2(vv)).astype(BF16)
    au = _bdot(pinv, jnp.concatenate([bd2(aq), bd2(makv)], axis=2)).astype(BF16)
    ahat = au[:, :, :LANES]
    uhat = au[:, :, LANES:]
    yield
    qhat = rq32 - _bdot(mrb, bd2(ahat))
    yhat = _bdot(jnp.concatenate([mrk, -mrb], axis=2),
                 jnp.concatenate([bd2(vv), bd2(uhat)], axis=1))
    yield
    bta = _bdot_tn(be, au)
    ktv = _bdot_tn(ke, vv)
    gmat = jnp.where(same_head, jnp.where(diag, gam_l, 0.0) - bta[:, :, :LANES], 0.0)
    hmat = jnp.where(same_head, ktv - bta[:, :, LANES:], 0.0)
    yield

    state = t_ref[...]
    y_rows = []
    for c in range(nc):
        sel = slice(c * RWKV_PAIRS, (c + 1) * RWKV_PAIRS)
        yc = _bdot(qhat[sel], state) + yhat[sel]
        y_rows.append(jnp.concatenate([yc[p] for p in range(RWKV_PAIRS)], axis=1))
        state = _bdot(gmat[sel], state) + hmat[sel]
        yield
    t_ref[...] = state

    y = jnp.concatenate(y_rows, axis=0)
    mean = _head_sum(y) * (1.0 / RWKV_HEAD_DIM)
    yc = y - mean
    var = _head_sum(yc * yc) * (1.0 / RWKV_HEAD_DIM)
    yn = yc * lax.rsqrt(var + RWKV_GN_EPS) * gng_ref[...] + gnb_ref[...]
    o_ref[0] = (yn + rw32_ref[0, :, 1 * W:2 * W]) * rw32_ref[0, :, 2 * W:3 * W]


def _ret_head(h, rt16_ref, rtg_ref, mask_ref, cd_ref, o_ref, r_ref, tb):
    c = RET_CHUNK
    nj = tb // c
    rows = [slice(j * c, (j + 1) * c) for j in range(nj)]

    def part(p, r):
        c0 = p * RET_QK_WIDTH + h * RET_QK_DIM
        return rt16_ref[0, r, c0:c0 + RET_QK_DIM]

    v0 = 4 * RET_QK_WIDTH + h * RET_V_DIM
    vb = [rt16_ref[0, r, v0:v0 + RET_V_DIM] for r in rows]
    scores = [_dot_nt(part(0, r), part(1, r)) for r in rows]
    ktv = [_dot_tn(part(2, rows[j]), vb[j]) for j in range(nj)]
    yield
    states = [r_ref[h]]
    for j in range(nj):
        states.append(states[j] * cd_ref[h] + ktv[j])
    r_ref[h] = states[nj]
    lhs = [jnp.concatenate([(scores[j] * mask_ref[h]).astype(BF16), part(3, rows[j])], axis=1)
           for j in range(nj)]
    rhs = [jnp.concatenate([vb[j], states[j].astype(BF16)], axis=0) for j in range(nj)]
    yield
    ys = [jnp.dot(lhs[j], rhs[j], preferred_element_type=F32) for j in range(nj)]
    yield
    for j in range(nj):
        y = ys[j]
        cols = slice(h * RET_V_DIM, (h + 1) * RET_V_DIM)
        g = rtg_ref[0, rows[j], cols]
        mu = jnp.mean(y, -1, keepdims=True)
        yc = y - mu
        var = jnp.mean(yc * yc, -1, keepdims=True)
        o_ref[0, rows[j], cols] = g * jax.nn.sigmoid(g) * (yc * lax.rsqrt(var + LN_EPS))
    yield


def _ret_body(rt16_ref, rtg_ref, mask_ref, cd_ref, o_ref, r_ref, *, tb):
    live = []
    pending = [_ret_head(h, rt16_ref, rtg_ref, mask_ref, cd_ref, o_ref, r_ref, tb) for h in range(RET_HEADS)]
    while live or pending:
        if pending:
            live.append(pending.pop(0))
        for gen in list(live):
            if next(gen, StopIteration) is StopIteration:
                live.remove(gen)
        yield


def _mixers_kernel(rw16_ref, rw32_ref, rt16_ref, rtg_ref, gng_ref, gnb_ref, mask_ref, cd_ref,
                   yr_ref, yt_ref, t_ref, r_ref, *, tb):
    @pl.when(pl.program_id(1) == 0)
    def _():
        t_ref[...] = jnp.zeros_like(t_ref)
        r_ref[...] = jnp.zeros_like(r_ref)

    rwkv = _rwkv_body(rw16_ref, rw32_ref, gng_ref, gnb_ref, yr_ref, t_ref, tb=tb)
    ret = _ret_body(rt16_ref, rtg_ref, mask_ref, cd_ref, yt_ref, r_ref, tb=tb)
    for _ in itertools.zip_longest(rwkv, ret):
        pass


def _mixers(rw16, rw32, rt16, rtg, gn_g, gn_b, mask, cd, *, tb=512):
    b, s, _ = rw16.shape
    c = RET_CHUNK
    tok = lambda w: pl.BlockSpec((1, tb, w), lambda i, j: (i, j, 0))
    vec = _const_spec((1, RWKV_WIDTH))
    return pl.pallas_call(
        functools.partial(_mixers_kernel, tb=tb),
        grid=(b, s // tb),
        in_specs=[
            tok(RW16_COLS), tok(RW32_COLS), tok(RT16_COLS), tok(RET_V_WIDTH),
            vec, vec,
            _const_spec((RET_HEADS, c, c)),
            pl.BlockSpec(memory_space=pltpu.SMEM),
        ],
        out_specs=[tok(RWKV_WIDTH), tok(RET_V_WIDTH)],
        out_shape=[jax.ShapeDtypeStruct((b, s, RWKV_WIDTH), F32),
                   jax.ShapeDtypeStruct((b, s, RET_V_WIDTH), F32)],
        scratch_shapes=[
            pltpu.VMEM((RWKV_PAIRS, LANES, LANES), F32),
            pltpu.VMEM((RET_HEADS, RET_QK_DIM, RET_V_DIM), F32),
        ],
        compiler_params=pltpu.CompilerParams(
            dimension_semantics=("arbitrary", "arbitrary"), vmem_limit_bytes=VMEM_LIMIT),
        name="mixers",
    )(rw16, rw32, rt16, rtg, gn_g, gn_b, mask, cd)


def _post_kernel(yr_ref, yt_ref, zg_ref, h_ref, p_ref, wbr_ref, wbt_ref, wo_ref, g2_ref, b2_ref,
                 win_ref, wout_ref, g3_ref, b3_ref, wg_ref, wp_ref, o_ref, acc_ref, *, fc):
    a = _dot(yr_ref[...], wbr_ref[...])
    b = _dot(yt_ref[...], wbt_ref[...])
    m = zg_ref[:, :D_MODEL] * a + zg_ref[:, D_MODEL:] * b
    h = _layer_norm(DN_ALPHA * h_ref[...] + _dot(m, wo_ref[...]), g2_ref[...], b2_ref[...])
    h = _ffn_block(h, win_ref, wout_ref, g3_ref, b3_ref, acc_ref, fc)
    gate = jax.nn.sigmoid(_dot(h, wg_ref[...]))
    o_ref[...] = h + gate * _dot(p_ref[...], wp_ref[...])


def _post(yr, yt, zg, h, p, wbr, wbt, wo, g2, b2, w_in, w_out, g3, b3, wg, wp, *, tm=512, fc=256):
    n = h.shape[0]
    tok = lambda w: pl.BlockSpec((tm, w), lambda i: (i, 0))
    vec = _const_spec((1, D_MODEL))
    return pl.pallas_call(
        functools.partial(_post_kernel, fc=fc),
        grid=(n // tm,),
        in_specs=[
            tok(RWKV_WIDTH), tok(RET_V_WIDTH), tok(GATE_IN), tok(D_MODEL), tok(PLE_DIM),
            _const_spec((RWKV_WIDTH, D_MODEL)),
            _const_spec((RET_V_WIDTH, D_MODEL)),
            _const_spec((D_MODEL, D_MODEL)),
            vec, vec,
            _const_spec((D_MODEL, 2 * D_FF)),
            _const_spec((D_FF, D_MODEL)),
            vec, vec,
            _const_spec((D_MODEL, D_MODEL)),
            _const_spec((PLE_DIM, D_MODEL)),
        ],
        out_specs=tok(D_MODEL),
        out_shape=jax.ShapeDtypeStruct((n, D_MODEL), F32),
        scratch_shapes=[pltpu.VMEM((tm, D_MODEL), F32)],
        compiler_params=pltpu.CompilerParams(
            dimension_semantics=("parallel",), vmem_limit_bytes=VMEM_LIMIT),
        name="post",
    )(yr, yt, zg, h, p, wbr, wbt, wo, g2, b2, w_in, w_out, g3, b3, wg, wp)


def _retention_constants(tm):
    h, c = RET_HEADS, RET_CHUNK
    log_gamma = jnp.log(1.0 - jnp.exp2(-5.0 - jnp.arange(h, dtype=F32)))
    idx = jnp.arange(c, dtype=F32)
    rel = idx[:, None] - idx[None, :]
    mask = jnp.where(rel >= 0, jnp.exp(log_gamma[:, None, None] * jnp.maximum(rel, 0.0)), 0.0)
    xi = jnp.exp(log_gamma[:, None] * (idx + 1.0))
    zeta = jnp.exp(log_gamma[:, None] * (c - 1.0 - idx))
    cd = jnp.exp(log_gamma * c)
    xi_t = jnp.broadcast_to(jnp.tile(xi, (1, tm // c))[:, :, None], (h, tm, RET_QK_DIM))
    zeta_t = jnp.broadcast_to(jnp.tile(zeta, (1, tm // c))[:, :, None], (h, tm, RET_QK_DIM))
    half = RET_QK_DIM // 2
    inv_freq = ROPE_BASE ** (-jnp.arange(half, dtype=F32) / half)
    freq2 = jnp.concatenate([inv_freq, inv_freq])[None, :]
    return freq2, mask, xi_t, zeta_t, cd


def kernel(x, p, positions, ln1_g, ln1_b, ffn1_w_in, ffn1_w_out, w_mix_in, rwkv_mu, rwkv_w0, rwkv_w_up,
           rwkv_a0, rwkv_a_up, rwkv_g_up, rwkv_k_k, rwkv_k_a, rwkv_r_k, rwkv_gn_g, rwkv_gn_b,
           w_branch_rwkv, w_branch_ret, w_mix_out, ln2_g, ln2_b, ffn2_w_in, ffn2_w_out, ln3_g, ln3_b,
           ple_w_proj, ple_w_gate):
    b, s, d = x.shape
    n = b * s
    bf = lambda w: w.astype(BF16)
    freq2, mask, xi_t, zeta_t, cd = _retention_constants(MIX_BLOCK)
    pos = positions.astype(F32).reshape(n, 1)
    h = x.reshape(n, d)
    for i in range(DEPTH):
        h = _ffn(h, bf(ffn1_w_in[i]), bf(ffn1_w_out[i]), ln1_g[i][None], ln1_b[i][None])
        zeros = jnp.zeros((DECAY_LORA, RWKV_WIDTH), F32)
        wwa = jnp.concatenate([
            jnp.concatenate([rwkv_w_up[i], zeros], axis=1),
            jnp.concatenate([zeros, rwkv_a_up[i]], axis=1)], axis=0)
        rw16, rw32, rt16, rtg, zg = _mix_in(
            h, bf(w_mix_in[i]), rwkv_mu[i][None], bf(wwa), rwkv_w0[i][None], rwkv_a0[i][None],
            bf(rwkv_g_up[i]), rwkv_k_k[i][None], rwkv_k_a[i][None], rwkv_r_k[i].reshape(1, RWKV_WIDTH),
            pos, freq2, xi_t, zeta_t, seq=s)
        yr, yt = _mixers(rw16.reshape(b, s, RW16_COLS), rw32.reshape(b, s, RW32_COLS),
                         rt16.reshape(b, s, RT16_COLS), rtg.reshape(b, s, RET_V_WIDTH),
                         rwkv_gn_g[i][None], rwkv_gn_b[i][None], mask, cd)
        h = _post(yr.reshape(n, RWKV_WIDTH), yt.reshape(n, RET_V_WIDTH), zg, h, p[i].reshape(n, PLE_DIM),
                  bf(w_branch_rwkv[i]), bf(w_branch_ret[i]), bf(w_mix_out[i]),
                  ln2_g[i][None], ln2_b[i][None],
                  bf(ffn2_w_in[i]), bf(ffn2_w_out[i]), ln3_g[i][None], ln3_b[i][None],
                  bf(ple_w_gate[i]), bf(ple_w_proj[i]))
    return h.reshape(b, s, d)
```

```python
import functools
import itertools
import math

import jax
import jax.numpy as jnp
from jax import lax
from jax.experimental import pallas as pl
from jax.experimental.pallas import tpu as pltpu

F32 = jnp.float32
BF16 = jnp.bfloat16

D_MODEL = 1024
PLE_DIM = 256
D_FF = 2816
RWKV_HEADS = 8
RWKV_HEAD_DIM = 64
RWKV_WIDTH = RWKV_HEADS * RWKV_HEAD_DIM
DECAY_LORA = 64
AAA_LORA = 64
GATE_LORA = 128
RWKV_GN_EPS = 64e-5
DECAY_SCALE = math.exp(-0.5)
RET_HEADS = 4
RET_QK_DIM = 128
RET_V_DIM = 256
RET_QK_WIDTH = RET_HEADS * RET_QK_DIM
RET_V_WIDTH = RET_HEADS * RET_V_DIM
RET_CHUNK = 128
ROPE_BASE = 10000.0
RWKV_IN = 3 * RWKV_WIDTH + DECAY_LORA + AAA_LORA + GATE_LORA
RET_IN = 2 * RET_QK_WIDTH + 2 * RET_V_WIDTH
GATE_IN = 2 * D_MODEL
DEPTH = 1
DN_ALPHA = (2 * DEPTH) ** 0.25
LN_EPS = 1e-5

LANES = 128
RWKV_CHUNK = 64
RWKV_PAIRS = RWKV_WIDTH // LANES
MXU_WIDTH = 256
RWKV_GROUP = MXU_WIDTH
RWKV_GROUPS = RWKV_WIDTH // RWKV_GROUP
VMEM_LIMIT = 56 * 1024 * 1024
MIX_BLOCK = 256

RW16_COLS = 6 * RWKV_WIDTH
RW32_COLS = 4 * RWKV_WIDTH
RT16_COLS = 4 * RET_QK_WIDTH + RET_V_WIDTH


def _dot(a, b):
    return jnp.dot(a.astype(BF16), b.astype(BF16), preferred_element_type=F32)


def _dot_nt(a, b):
    return lax.dot_general(a.astype(BF16), b.astype(BF16), (((1,), (1,)), ((), ())),
                           preferred_element_type=F32)


def _dot_tn(a, b):
    return lax.dot_general(a.astype(BF16), b.astype(BF16), (((0,), (0,)), ((), ())),
                           preferred_element_type=F32)


def _bdot(a, b):
    return lax.dot_general(a.astype(BF16), b.astype(BF16), (((2,), (1,)), ((0,), (0,))),
                           preferred_element_type=F32)


def _bdot_nt(a, b):
    return lax.dot_general(a.astype(BF16), b.astype(BF16), (((2,), (2,)), ((0,), (0,))),
                           preferred_element_type=F32)


def _bdot_tn(a, b):
    return lax.dot_general(a.astype(BF16), b.astype(BF16), (((1,), (1,)), ((0,), (0,))),
                           preferred_element_type=F32)


def _dot_01(x, m01, *, left=False):
    hi = x.astype(BF16)
    lo = (x - hi.astype(F32)).astype(BF16)
    if left:
        dot = lambda t: jnp.dot(m01, t, preferred_element_type=F32)
    else:
        dot = lambda t: jnp.dot(t, m01, preferred_element_type=F32)
    return dot(hi) + dot(lo)


HEAD_SHIFT = RWKV_HEAD_DIM.bit_length() - 1


def _same_head_mask(width):
    jr = lax.broadcasted_iota(jnp.int32, (width, width), 0)
    jc = lax.broadcasted_iota(jnp.int32, (width, width), 1)
    return (jr >> HEAD_SHIFT) == (jc >> HEAD_SHIFT), jr == jc


def _head_sum(x):
    same_head, _ = _same_head_mask(RWKV_GROUP)
    ones_bd = jnp.where(same_head, 1.0, 0.0).astype(BF16)
    return jnp.concatenate(
        [jnp.dot(x[:, q * RWKV_GROUP:(q + 1) * RWKV_GROUP].astype(BF16), ones_bd, preferred_element_type=F32)
         for q in range(RWKV_GROUPS)], axis=1)


def _layer_norm(y, g, b):
    mu = jnp.mean(y, -1, keepdims=True)
    yc = y - mu
    var = jnp.mean(yc * yc, -1, keepdims=True)
    return yc * lax.rsqrt(var + LN_EPS) * g + b


def _const_spec(shape):
    return pl.BlockSpec(shape, lambda *_: (0,) * len(shape), pipeline_mode=pl.Buffered(1))


def _ffn_block(x, win_ref, wout_ref, g_ref, b_ref, acc_ref, fc):
    xb = x.astype(BF16)
    for j in range(D_FF // fc):
        gate = jnp.dot(xb, win_ref[:, j * fc:(j + 1) * fc], preferred_element_type=F32)
        up = jnp.dot(xb, win_ref[:, D_FF + j * fc:D_FF + (j + 1) * fc], preferred_element_type=F32)
        act = (gate * jax.nn.sigmoid(gate) * up).astype(BF16)
        part = jnp.dot(act, wout_ref[j * fc:(j + 1) * fc, :], preferred_element_type=F32)
        if j == 0:
            acc_ref[...] = part
        else:
            acc_ref[...] += part
    y = DN_ALPHA * x + 0.5 * acc_ref[...]
    return _layer_norm(y, g_ref[...], b_ref[...])


def _ffn_kernel(x_ref, win_ref, wout_ref, g_ref, b_ref, o_ref, acc_ref, *, fc):
    o_ref[...] = _ffn_block(x_ref[...], win_ref, wout_ref, g_ref, b_ref, acc_ref, fc)


def _ffn(x, w_in, w_out, g, b, *, tm=512, fc=256):
    n = x.shape[0]
    return pl.pallas_call(
        functools.partial(_ffn_kernel, fc=fc),
        grid=(n // tm,),
        in_specs=[
            pl.BlockSpec((tm, D_MODEL), lambda i: (i, 0)),
            _const_spec((D_MODEL, 2 * D_FF)),
            _const_spec((D_FF, D_MODEL)),
            _const_spec((1, D_MODEL)),
            _const_spec((1, D_MODEL)),
        ],
        out_specs=pl.BlockSpec((tm, D_MODEL), lambda i: (i, 0)),
        out_shape=jax.ShapeDtypeStruct((n, D_MODEL), F32),
        scratch_shapes=[pltpu.VMEM((tm, D_MODEL), F32)],
        compiler_params=pltpu.CompilerParams(
            dimension_semantics=("parallel",), vmem_limit_bytes=VMEM_LIMIT),
        name="ffn",
    )(x, w_in, w_out, g, b)


def _mixin_kernel(h_ref, w_ref, tril_ref, mu_ref, wwa_ref, w0_ref, a0_ref, gup_ref, kk_ref, ka_ref, rk_ref,
                  pos_ref, freq_ref, xi_ref, zeta_ref,
                  rw16_ref, rw32_ref, rt16_ref, rtg_ref, zg_ref, carry_ref, *, tm, tiles_per_seq):
    L = RWKV_CHUNK
    W = RWKV_WIDTH
    nc = tm // L

    @pl.when(pl.program_id(0) % tiles_per_seq == 0)
    def _():
        carry_ref[...] = jnp.zeros_like(carry_ref)

    hb = h_ref[...].astype(BF16)

    def proj(lo, width):
        return jnp.dot(hb, w_ref[:, lo:lo + width], preferred_element_type=F32)

    z = proj(0, RWKV_IN)
    row = lax.broadcasted_iota(jnp.int32, z.shape, 0)
    prev = jnp.where(row == 0, carry_ref[...], pltpu.roll(z, 1, axis=0))
    carry_ref[...] = z[tm - 1:tm, :]
    zs = z + (prev - z) * mu_ref[...]

    r = zs[:, :W]
    k = zs[:, W:2 * W]
    v = zs[:, 2 * W:3 * W]
    dwa = zs[:, 3 * W:3 * W + LANES]
    dg = zs[:, 3 * W + LANES:]
    lane = lax.broadcasted_iota(jnp.int32, (tm, LANES), 1)
    dwa = jnp.where(lane < RWKV_HEAD_DIM, jnp.tanh(dwa), dwa)

    zqk = proj(RWKV_IN, 2 * RET_QK_WIDTH)

    wa = _dot(dwa, wwa_ref[...])
    g = _dot(jax.nn.sigmoid(dg), gup_ref[...])
    lw = -DECAY_SCALE * jax.nn.sigmoid(w0_ref[...] + wa[:, :W])
    a = jax.nn.sigmoid(a0_ref[...] + wa[:, W:])

    zg_ref[:, :D_MODEL] = jax.nn.sigmoid(proj(RWKV_IN + RET_IN, D_MODEL))

    cs = _dot_01(lw, tril_ref[...], left=True)

    zg_ref[:, D_MODEL:] = jax.nn.sigmoid(proj(RWKV_IN + RET_IN + D_MODEL, D_MODEL))

    kk = k * kk_ref[...]
    kk = kk * lax.rsqrt(jnp.maximum(_head_sum(kk * kk), 1e-24))
    k2 = k * (1.0 + (a - 1.0) * ka_ref[...])
    bonus = _head_sum(r * k2 * rk_ref[...]) * v

    zv = proj(RWKV_IN + 2 * RET_QK_WIDTH, RET_V_WIDTH)
    zgr = proj(RWKV_IN + 2 * RET_QK_WIDTH + RET_V_WIDTH, RET_V_WIDTH)
    rtg_ref[...] = zgr * jax.nn.sigmoid(zgr)

    gam_l = jnp.concatenate(
        [jnp.broadcast_to(jnp.exp(cs[c * L + L - 1:c * L + L, :]), (L, W)) for c in range(nc)], axis=0)
    ka = kk * a
    inv = jnp.exp(-cs)
    end = gam_l * inv
    rw16_ref[:, 0 * W:1 * W] = (kk * jnp.exp(cs - lw)).astype(BF16)
    rw16_ref[:, 1 * W:2 * W] = (k2 * inv).astype(BF16)
    rw16_ref[:, 2 * W:3 * W] = (ka * inv).astype(BF16)
    rw16_ref[:, 3 * W:4 * W] = (k2 * end).astype(BF16)
    rw16_ref[:, 4 * W:5 * W] = (ka * end).astype(BF16)
    rw16_ref[:, 5 * W:6 * W] = v.astype(BF16)
    rw32_ref[:, 0 * W:1 * W] = r * jnp.exp(cs)
    rw32_ref[:, 1 * W:2 * W] = bonus
    rw32_ref[:, 2 * W:3 * W] = g
    rw32_ref[:, 3 * W:4 * W] = gam_l

    ang = pos_ref[...] * freq_ref[...]
    cos2 = jnp.cos(ang)
    sin2 = jnp.sin(ang)
    sin2 = jnp.where(lane < RET_QK_DIM // 2, -sin2, sin2)

    def rot(x):
        return x * cos2 + pltpu.roll(x, RET_QK_DIM // 2, axis=1) * sin2

    for h in range(RET_HEADS):
        cols = slice(h * RET_QK_DIM, (h + 1) * RET_QK_DIM)
        q = rot(zqk[:, cols])
        kr = rot(zqk[:, RET_QK_WIDTH + h * RET_QK_DIM:RET_QK_WIDTH + (h + 1) * RET_QK_DIM])
        kr = kr * (RET_QK_DIM ** -0.5)
        for part, val in enumerate((q, kr, kr * zeta_ref[h], q * xi_ref[h])):
            c0 = part * RET_QK_WIDTH + h * RET_QK_DIM
            rt16_ref[:, c0:c0 + RET_QK_DIM] = val.astype(BF16)
    rt16_ref[:, 4 * RET_QK_WIDTH:] = zv.astype(BF16)


def _mix_in(h, w, mu, wwa, w0, a0, g_up, k_k, k_a, r_k, pos, freq2, xi_t, zeta_t, *, seq, tm=MIX_BLOCK):
    n = h.shape[0]
    W = RWKV_WIDTH
    t = jnp.arange(tm)
    tril = ((t[:, None] // RWKV_CHUNK == t[None, :] // RWKV_CHUNK) & (t[None, :] <= t[:, None])).astype(BF16)
    tok = lambda width: pl.BlockSpec((tm, width), lambda i: (i, 0))
    vec = _const_spec((1, W))
    widths = (RW16_COLS, RW32_COLS, RT16_COLS, RET_V_WIDTH, GATE_IN)
    dtypes = (BF16, F32, BF16, F32, F32)
    return pl.pallas_call(
        functools.partial(_mixin_kernel, tm=tm, tiles_per_seq=seq // tm),
        grid=(n // tm,),
        in_specs=[
            tok(D_MODEL),
            _const_spec((D_MODEL, RWKV_IN + RET_IN + GATE_IN)),
            _const_spec((tm, tm)),
            _const_spec((1, RWKV_IN)),
            _const_spec((LANES, 2 * W)),
            vec, vec,
            _const_spec((GATE_LORA, W)),
            vec, vec, vec,
            tok(1),
            _const_spec((1, LANES)),
            _const_spec((RET_HEADS, tm, RET_QK_DIM)),
            _const_spec((RET_HEADS, tm, RET_QK_DIM)),
        ],
        out_specs=[tok(width) for width in widths],
        out_shape=[jax.ShapeDtypeStruct((n, width), dt) for width, dt in zip(widths, dtypes)],
        scratch_shapes=[pltpu.VMEM((1, RWKV_IN), F32)],
        compiler_params=pltpu.CompilerParams(
            dimension_semantics=("arbitrary",), vmem_limit_bytes=VMEM_LIMIT),
        name="mix_in",
    )(h, w, tril, mu, wwa, w0, a0, g_up, k_k, k_a, r_k, pos, freq2, xi_t, zeta_t)


def _rwkv_batched(rw16_ref, rw32_ref, chunks, out):
    L = RWKV_CHUNK
    W = RWKV_WIDTH

    def tiles(ref, col0, rows=L):
        return jnp.stack([ref[0, c * L:c * L + rows, col0 + p * LANES:col0 + (p + 1) * LANES]
                          for c in chunks for p in range(RWKV_PAIRS)])

    aq = tiles(rw16_ref, 0 * W)
    kd = tiles(rw16_ref, 1 * W)
    bd = tiles(rw16_ref, 2 * W)
    ke = tiles(rw16_ref, 3 * W)
    be = tiles(rw16_ref, 4 * W)
    vv = tiles(rw16_ref, 5 * W)
    rq32 = tiles(rw32_ref, 0 * W)
    rq = rq32.astype(BF16)
    gam_l = tiles(rw32_ref, 3 * W, rows=1)

    same_head, diag = _same_head_mask(LANES)
    trow = lax.broadcasted_iota(jnp.int32, (L, LANES), 0)
    scol = lax.broadcasted_iota(jnp.int32, (L, LANES), 1)
    m_left = scol < RWKV_HEAD_DIM
    scol = scol & (L - 1)
    strict = scol < trow
    incl = scol <= trow
    eye_pair = jnp.where(scol == trow, 1.0, 0.0).astype(F32)

    def bd2(x):
        zero = jnp.zeros_like(x)
        return jnp.concatenate([jnp.where(m_left, x, zero), jnp.where(m_left, zero, x)], axis=1)

    m = _bdot_nt(jnp.concatenate([aq, rq], axis=1),
                 jnp.concatenate([bd2(kd), bd2(bd)], axis=1))
    mak = jnp.where(strict, m[:, :L, :LANES], 0.0).astype(BF16)
    mab = jnp.where(strict, m[:, :L, LANES:], 0.0)
    mrk = jnp.where(incl, m[:, L:, :LANES], 0.0).astype(BF16)
    mrb = jnp.where(incl, m[:, L:, LANES:], 0.0).astype(BF16)
    yield

    npow = mab.astype(BF16)
    pinv = eye_pair - mab
    for _ in range(5):
        npow = _bdot(npow, bd2(npow)).astype(BF16)
        pinv = pinv + _bdot(pinv, bd2(npow))
        yield

    makv = _bdot(mak, bd2(vv)).astype(BF16)
    au = _bdot(pinv, jnp.concatenate([bd2(aq), bd2(makv)], axis=2)).astype(BF16)
    ahat = au[:, :, :LANES]
    uhat = au[:, :, LANES:]
    yield
    out["qhat"] = rq32 - _bdot(mrb, bd2(ahat))
    out["yhat"] = _bdot(jnp.concatenate([mrk, -mrb], axis=2),
                        jnp.concatenate([bd2(vv), bd2(uhat)], axis=1))
    yield
    bta = _bdot_tn(be, au)
    ktv = _bdot_tn(ke, vv)
    out["gmat"] = jnp.where(same_head, jnp.where(diag, gam_l, 0.0) - bta[:, :, :LANES], 0.0)
    out["hmat"] = jnp.where(same_head, ktv - bta[:, :, LANES:], 0.0)
    yield


def _rwkv_chain(state, res, y_rows):
    for c in range(res["qhat"].shape[0] // RWKV_PAIRS):
        sel = slice(c * RWKV_PAIRS, (c + 1) * RWKV_PAIRS)
        yc = _bdot(res["qhat"][sel], state[0]) + res["yhat"][sel]
        y_rows.append(jnp.concatenate([yc[p] for p in range(RWKV_PAIRS)], axis=1))
        state[0] = _bdot(res["gmat"][sel], state[0]) + res["hmat"][sel]
        yield


def _rwkv_finish(y_rows, rw32_ref, gng_ref, gnb_ref, o_ref):
    W = RWKV_WIDTH
    y = jnp.concatenate(y_rows, axis=0)
    mean = _head_sum(y) * (1.0 / RWKV_HEAD_DIM)
    yc = y - mean
    var = _head_sum(yc * yc) * (1.0 / RWKV_HEAD_DIM)
    yn = yc * lax.rsqrt(var + RWKV_GN_EPS) * gng_ref[...] + gnb_ref[...]
    o_ref[0] = (yn + rw32_ref[0, :, 1 * W:2 * W]) * rw32_ref[0, :, 2 * W:3 * W]


def _ret_head(h, rt16_ref, rtg_ref, mask_ref, cd_ref, o_ref, r_ref, tb):
    c = RET_CHUNK
    nj = tb // c
    rows = [slice(j * c, (j + 1) * c) for j in range(nj)]

    def part(p, r):
        c0 = p * RET_QK_WIDTH + h * RET_QK_DIM
        return rt16_ref[0, r, c0:c0 + RET_QK_DIM]

    v0 = 4 * RET_QK_WIDTH + h * RET_V_DIM
    vb = [rt16_ref[0, r, v0:v0 + RET_V_DIM] for r in rows]
    scores = [_dot_nt(part(0, r), part(1, r)) for r in rows]
    ktv = [_dot_tn(part(2, rows[j]), vb[j]) for j in range(nj)]
    yield
    states = [r_ref[h]]
    for j in range(nj):
        states.append(states[j] * cd_ref[h] + ktv[j])
    r_ref[h] = states[nj]
    lhs = [jnp.concatenate([(scores[j] * mask_ref[h]).astype(BF16), part(3, rows[j])], axis=1)
           for j in range(nj)]
    rhs = [jnp.concatenate([vb[j], states[j].astype(BF16)], axis=0) for j in range(nj)]
    yield
    ys = [jnp.dot(lhs[j], rhs[j], preferred_element_type=F32) for j in range(nj)]
    yield
    for j in range(nj):
        y = ys[j]
        cols = slice(h * RET_V_DIM, (h + 1) * RET_V_DIM)
        g = rtg_ref[0, rows[j], cols]
        mu = jnp.mean(y, -1, keepdims=True)
        yc = y - mu
        var = jnp.mean(yc * yc, -1, keepdims=True)
        o_ref[0, rows[j], cols] = g * (yc * lax.rsqrt(var + LN_EPS))
    yield


def _ret_body(rt16_ref, rtg_ref, mask_ref, cd_ref, o_ref, r_ref, *, tb):
    live = []
    pending = [_ret_head(h, rt16_ref, rtg_ref, mask_ref, cd_ref, o_ref, r_ref, tb) for h in range(RET_HEADS)]
    while live or pending:
        if pending:
            live.append(pending.pop(0))
        for gen in list(live):
            if next(gen, StopIteration) is StopIteration:
                live.remove(gen)
        yield


def _mixers_kernel(rw16_ref, rw32_ref, rt16_ref, rtg_ref, gng_ref, gnb_ref, mask_ref, cd_ref,
                   yr_ref, yt_ref, t_ref, r_ref, *, tb):
    @pl.when(pl.program_id(1) == 0)
    def _():
        t_ref[...] = jnp.zeros_like(t_ref)
        r_ref[...] = jnp.zeros_like(r_ref)

    nc = tb // RWKV_CHUNK
    first, second = {}, {}
    for _ in _rwkv_batched(rw16_ref, rw32_ref, range(nc // 2), first):
        pass
    state = [t_ref[...]]
    y_rows = []
    for _ in itertools.zip_longest(_rwkv_chain(state, first, y_rows),
                                   _rwkv_batched(rw16_ref, rw32_ref, range(nc // 2, nc), second)):
        pass
    for _ in itertools.zip_longest(_rwkv_chain(state, second, y_rows),
                                   _ret_body(rt16_ref, rtg_ref, mask_ref, cd_ref, yt_ref, r_ref, tb=tb)):
        pass
    t_ref[...] = state[0]
    _rwkv_finish(y_rows, rw32_ref, gng_ref, gnb_ref, yr_ref)


def _mixers(rw16, rw32, rt16, rtg, gn_g, gn_b, mask, cd, *, tb=512):
    b, s, _ = rw16.shape
    c = RET_CHUNK
    tok = lambda w: pl.BlockSpec((1, tb, w), lambda i, j: (i, j, 0))
    vec = _const_spec((1, RWKV_WIDTH))
    return pl.pallas_call(
        functools.partial(_mixers_kernel, tb=tb),
        grid=(b, s // tb),
        in_specs=[
            tok(RW16_COLS), tok(RW32_COLS), tok(RT16_COLS), tok(RET_V_WIDTH),
            vec, vec,
            _const_spec((RET_HEADS, c, c)),
            pl.BlockSpec(memory_space=pltpu.SMEM),
        ],
        out_specs=[tok(RWKV_WIDTH), tok(RET_V_WIDTH)],
        out_shape=[jax.ShapeDtypeStruct((b, s, RWKV_WIDTH), F32),
                   jax.ShapeDtypeStruct((b, s, RET_V_WIDTH), F32)],
        scratch_shapes=[
            pltpu.VMEM((RWKV_PAIRS, LANES, LANES), F32),
            pltpu.VMEM((RET_HEADS, RET_QK_DIM, RET_V_DIM), F32),
        ],
        compiler_params=pltpu.CompilerParams(
            dimension_semantics=("arbitrary", "arbitrary"), vmem_limit_bytes=VMEM_LIMIT),
        name="mixers",
    )(rw16, rw32, rt16, rtg, gn_g, gn_b, mask, cd)


def _post_kernel(yr_ref, yt_ref, zg_ref, h_ref, p_ref, wbr_ref, wbt_ref, wo_ref, g2_ref, b2_ref,
                 win_ref, wout_ref, g3_ref, b3_ref, wg_ref, wp_ref, o_ref, acc_ref, *, fc):
    a = _dot(yr_ref[...], wbr_ref[...])
    b = _dot(yt_ref[...], wbt_ref[...])
    m = zg_ref[:, :D_MODEL] * a + zg_ref[:, D_MODEL:] * b
    h = _layer_norm(DN_ALPHA * h_ref[...] + _dot(m, wo_ref[...]), g2_ref[...], b2_ref[...])
    h = _ffn_block(h, win_ref, wout_ref, g3_ref, b3_ref, acc_ref, fc)
    gate = jax.nn.sigmoid(_dot(h, wg_ref[...]))
    o_ref[...] = h + gate * _dot(p_ref[...], wp_ref[...])


def _post(yr, yt, zg, h, p, wbr, wbt, wo, g2, b2, w_in, w_out, g3, b3, wg, wp, *, tm=512, fc=256):
    n = h.shape[0]
    tok = lambda w: pl.BlockSpec((tm, w), lambda i: (i, 0))
    vec = _const_spec((1, D_MODEL))
    return pl.pallas_call(
        functools.partial(_post_kernel, fc=fc),
        grid=(n // tm,),
        in_specs=[
            tok(RWKV_WIDTH), tok(RET_V_WIDTH), tok(GATE_IN), tok(D_MODEL), tok(PLE_DIM),
            _const_spec((RWKV_WIDTH, D_MODEL)),
            _const_spec((RET_V_WIDTH, D_MODEL)),
            _const_spec((D_MODEL, D_MODEL)),
            vec, vec,
            _const_spec((D_MODEL, 2 * D_FF)),
            _const_spec((D_FF, D_MODEL)),
            vec, vec,
            _const_spec((D_MODEL, D_MODEL)),
            _const_spec((PLE_DIM, D_MODEL)),
        ],
        out_specs=tok(D_MODEL),
        out_shape=jax.ShapeDtypeStruct((n, D_MODEL), F32),
        scratch_shapes=[pltpu.VMEM((tm, D_MODEL), F32)],
        compiler_params=pltpu.CompilerParams(
            dimension_semantics=("parallel",), vmem_limit_bytes=VMEM_LIMIT),
        name="post",
    )(yr, yt, zg, h, p, wbr, wbt, wo, g2, b2, w_in, w_out, g3, b3, wg, wp)


def _retention_constants(tm):
    h, c = RET_HEADS, RET_CHUNK
    log_gamma = jnp.log(1.0 - jnp.exp2(-5.0 - jnp.arange(h, dtype=F32)))
    idx = jnp.arange(c, dtype=F32)
    rel = idx[:, None] - idx[None, :]
    mask = jnp.where(rel >= 0, jnp.exp(log_gamma[:, None, None] * jnp.maximum(rel, 0.0)), 0.0)
    xi = jnp.exp(log_gamma[:, None] * (idx + 1.0))
    zeta = jnp.exp(log_gamma[:, None] * (c - 1.0 - idx))
    cd = jnp.exp(log_gamma * c)
    xi_t = jnp.broadcast_to(jnp.tile(xi, (1, tm // c))[:, :, None], (h, tm, RET_QK_DIM))
    zeta_t = jnp.broadcast_to(jnp.tile(zeta, (1, tm // c))[:, :, None], (h, tm, RET_QK_DIM))
    half = RET_QK_DIM // 2
    inv_freq = ROPE_BASE ** (-jnp.arange(half, dtype=F32) / half)
    freq2 = jnp.concatenate([inv_freq, inv_freq])[None, :]
    return freq2, mask, xi_t, zeta_t, cd


def kernel(x, p, positions, ln1_g, ln1_b, ffn1_w_in, ffn1_w_out, w_mix_in, rwkv_mu, rwkv_w0, rwkv_w_up,
           rwkv_a0, rwkv_a_up, rwkv_g_up, rwkv_k_k, rwkv_k_a, rwkv_r_k, rwkv_gn_g, rwkv_gn_b,
           w_branch_rwkv, w_branch_ret, w_mix_out, ln2_g, ln2_b, ffn2_w_in, ffn2_w_out, ln3_g, ln3_b,
           ple_w_proj, ple_w_gate):
    b, s, d = x.shape
    n = b * s
    bf = lambda w: w.astype(BF16)
    freq2, mask, xi_t, zeta_t, cd = _retention_constants(MIX_BLOCK)
    pos = positions.astype(F32).reshape(n, 1)
    h = x.reshape(n, d)
    for i in range(DEPTH):
        h = _ffn(h, bf(ffn1_w_in[i]), bf(ffn1_w_out[i]), ln1_g[i][None], ln1_b[i][None])
        zeros = jnp.zeros((DECAY_LORA, RWKV_WIDTH), F32)
        wwa = jnp.concatenate([
            jnp.concatenate([rwkv_w_up[i], zeros], axis=1),
            jnp.concatenate([zeros, rwkv_a_up[i]], axis=1)], axis=0)
        rw16, rw32, rt16, rtg, zg = _mix_in(
            h, bf(w_mix_in[i]), rwkv_mu[i][None], bf(wwa), rwkv_w0[i][None], rwkv_a0[i][None],
            bf(rwkv_g_up[i]), rwkv_k_k[i][None], rwkv_k_a[i][None], rwkv_r_k[i].reshape(1, RWKV_WIDTH),
            pos, freq2, xi_t, zeta_t, seq=s)
        yr, yt = _mixers(rw16.reshape(b, s, RW16_COLS), rw32.reshape(b, s, RW32_COLS),
                         rt16.reshape(b, s, RT16_COLS), rtg.reshape(b, s, RET_V_WIDTH),
                         rwkv_gn_g[i][None], rwkv_gn_b[i][None], mask, cd)
        h = _post(yr.reshape(n, RWKV_WIDTH), yt.reshape(n, RET_V_WIDTH), zg, h, p[i].reshape(n, PLE_DIM),
                  bf(w_branch_rwkv[i]), bf(w_branch_ret[i]), bf(w_mix_out[i]),
                  ln2_g[i][None], ln2_b[i][None],
                  bf(ffn2_w_in[i]), bf(ffn2_w_out[i]), ln3_g[i][None], ln3_b[i][None],
                  bf(ple_w_gate[i]), bf(ple_w_proj[i]))
    return h.reshape(b, s, d)
```

```python
import functools
import itertools
import math

import jax
import jax.numpy as jnp
from jax import lax
from jax.experimental import pallas as pl
from jax.experimental.pallas import tpu as pltpu

F32 = jnp.float32
BF16 = jnp.bfloat16

D_MODEL = 1024
PLE_DIM = 256
D_FF = 2816
RWKV_HEADS = 8
RWKV_HEAD_DIM = 64
RWKV_WIDTH = RWKV_HEADS * RWKV_HEAD_DIM
DECAY_LORA = 64
AAA_LORA = 64
GATE_LORA = 128
RWKV_GN_EPS = 64e-5
DECAY_SCALE = math.exp(-0.5)
RET_HEADS = 4
RET_QK_DIM = 128
RET_V_DIM = 256
RET_QK_WIDTH = RET_HEADS * RET_QK_DIM
RET_V_WIDTH = RET_HEADS * RET_V_DIM
RET_CHUNK = 128
ROPE_BASE = 10000.0
RWKV_IN = 3 * RWKV_WIDTH + DECAY_LORA + AAA_LORA + GATE_LORA
RET_IN = 2 * RET_QK_WIDTH + 2 * RET_V_WIDTH
GATE_IN = 2 * D_MODEL
DEPTH = 1
DN_ALPHA = (2 * DEPTH) ** 0.25
LN_EPS = 1e-5

LANES = 128
RWKV_CHUNK = 64
RWKV_PAIRS = RWKV_WIDTH // LANES
MXU_WIDTH = 256
RWKV_GROUP = MXU_WIDTH
RWKV_GROUPS = RWKV_WIDTH // RWKV_GROUP
VMEM_LIMIT = 56 * 1024 * 1024
MIX_BLOCK = 256

RW16_COLS = 6 * RWKV_WIDTH
RW32_COLS = 4 * RWKV_WIDTH
RT16_COLS = 4 * RET_QK_WIDTH + RET_V_WIDTH


def _dot(a, b):
    return jnp.dot(a.astype(BF16), b.astype(BF16), preferred_element_type=F32)


def _dot_nt(a, b):
    return lax.dot_general(a.astype(BF16), b.astype(BF16), (((1,), (1,)), ((), ())),
                           preferred_element_type=F32)


def _dot_tn(a, b):
    return lax.dot_general(a.astype(BF16), b.astype(BF16), (((0,), (0,)), ((), ())),
                           preferred_element_type=F32)


def _bdot(a, b):
    return lax.dot_general(a.astype(BF16), b.astype(BF16), (((2,), (1,)), ((0,), (0,))),
                           preferred_element_type=F32)


def _bdot_nt(a, b):
    return lax.dot_general(a.astype(BF16), b.astype(BF16), (((2,), (2,)), ((0,), (0,))),
                           preferred_element_type=F32)


def _bdot_tn(a, b):
    return lax.dot_general(a.astype(BF16), b.astype(BF16), (((1,), (1,)), ((0,), (0,))),
                           preferred_element_type=F32)


def _dot_01(x, m01, *, left=False):
    hi = x.astype(BF16)
    lo = (x - hi.astype(F32)).astype(BF16)
    if left:
        dot = lambda t: jnp.dot(m01, t, preferred_element_type=F32)
    else:
        dot = lambda t: jnp.dot(t, m01, preferred_element_type=F32)
    return dot(hi) + dot(lo)


HEAD_SHIFT = RWKV_HEAD_DIM.bit_length() - 1


def _same_head_mask(width):
    jr = lax.broadcasted_iota(jnp.int32, (width, width), 0)
    jc = lax.broadcasted_iota(jnp.int32, (width, width), 1)
    return (jr >> HEAD_SHIFT) == (jc >> HEAD_SHIFT), jr == jc


def _head_sum(x):
    same_head, _ = _same_head_mask(RWKV_GROUP)
    ones_bd = jnp.where(same_head, 1.0, 0.0).astype(BF16)
    return jnp.concatenate(
        [jnp.dot(x[:, q * RWKV_GROUP:(q + 1) * RWKV_GROUP].astype(BF16), ones_bd, preferred_element_type=F32)
         for q in range(RWKV_GROUPS)], axis=1)


def _layer_norm(y, g, b):
    mu = jnp.mean(y, -1, keepdims=True)
    yc = y - mu
    var = jnp.mean(yc * yc, -1, keepdims=True)
    return yc * lax.rsqrt(var + LN_EPS) * g + b


def _const_spec(shape):
    return pl.BlockSpec(shape, lambda *_: (0,) * len(shape), pipeline_mode=pl.Buffered(1))


def _ffn_block(x, win_ref, wout_ref, g_ref, b_ref, acc_ref, fc):
    xb = x.astype(BF16)
    for j in range(D_FF // fc):
        gate = jnp.dot(xb, win_ref[:, j * fc:(j + 1) * fc], preferred_element_type=F32)
        up = jnp.dot(xb, win_ref[:, D_FF + j * fc:D_FF + (j + 1) * fc], preferred_element_type=F32)
        act = (gate * jax.nn.sigmoid(gate) * up).astype(BF16)
        part = jnp.dot(act, wout_ref[j * fc:(j + 1) * fc, :], preferred_element_type=F32)
        if j == 0:
            acc_ref[...] = part
        else:
            acc_ref[...] += part
    y = DN_ALPHA * x + 0.5 * acc_ref[...]
    return _layer_norm(y, g_ref[...], b_ref[...])


def _ffn_kernel(x_ref, win_ref, wout_ref, g_ref, b_ref, o_ref, acc_ref, *, fc):
    o_ref[...] = _ffn_block(x_ref[...], win_ref, wout_ref, g_ref, b_ref, acc_ref, fc)


def _ffn(x, w_in, w_out, g, b, *, tm=512, fc=256):
    n = x.shape[0]
    return pl.pallas_call(
        functools.partial(_ffn_kernel, fc=fc),
        grid=(n // tm,),
        in_specs=[
            pl.BlockSpec((tm, D_MODEL), lambda i: (i, 0)),
            _const_spec((D_MODEL, 2 * D_FF)),
            _const_spec((D_FF, D_MODEL)),
            _const_spec((1, D_MODEL)),
            _const_spec((1, D_MODEL)),
        ],
        out_specs=pl.BlockSpec((tm, D_MODEL), lambda i: (i, 0)),
        out_shape=jax.ShapeDtypeStruct((n, D_MODEL), F32),
        scratch_shapes=[pltpu.VMEM((tm, D_MODEL), F32)],
        compiler_params=pltpu.CompilerParams(
            dimension_semantics=("parallel",), vmem_limit_bytes=VMEM_LIMIT),
        name="ffn",
    )(x, w_in, w_out, g, b)


def _mixin_kernel(h_ref, w_ref, tril_ref, mu_ref, wwa_ref, w0_ref, a0_ref, gup_ref, kk_ref, ka_ref, rk_ref,
                  pos_ref, freq_ref, xi_ref, zeta_ref,
                  rw16_ref, rw32_ref, rt16_ref, rtg_ref, zg_ref, carry_ref, *, tm, tiles_per_seq):
    L = RWKV_CHUNK
    W = RWKV_WIDTH
    nc = tm // L

    @pl.when(pl.program_id(0) % tiles_per_seq == 0)
    def _():
        carry_ref[...] = jnp.zeros_like(carry_ref)

    hb = h_ref[...].astype(BF16)

    def proj(lo, width):
        return jnp.dot(hb, w_ref[:, lo:lo + width], preferred_element_type=F32)

    z = proj(0, RWKV_IN)
    row = lax.broadcasted_iota(jnp.int32, z.shape, 0)
    prev = jnp.where(row == 0, carry_ref[...], pltpu.roll(z, 1, axis=0))
    carry_ref[...] = z[tm - 1:tm, :]
    zs = z + (prev - z) * mu_ref[...]

    r = zs[:, :W]
    k = zs[:, W:2 * W]
    v = zs[:, 2 * W:3 * W]
    dwa = zs[:, 3 * W:3 * W + LANES]
    dg = zs[:, 3 * W + LANES:]
    lane = lax.broadcasted_iota(jnp.int32, (tm, LANES), 1)
    dwa = jnp.where(lane < RWKV_HEAD_DIM, jnp.tanh(dwa), dwa)

    zqk = proj(RWKV_IN, 2 * RET_QK_WIDTH)

    wa = _dot(dwa, wwa_ref[...])
    g = _dot(jax.nn.sigmoid(dg), gup_ref[...])
    lw = -DECAY_SCALE * jax.nn.sigmoid(w0_ref[...] + wa[:, :W])
    a = jax.nn.sigmoid(a0_ref[...] + wa[:, W:])

    zg_ref[:, :D_MODEL] = proj(RWKV_IN + RET_IN, D_MODEL)

    cs = _dot_01(lw, tril_ref[...], left=True)

    zg_ref[:, D_MODEL:] = proj(RWKV_IN + RET_IN + D_MODEL, D_MODEL)

    kk = k * kk_ref[...]
    kk = kk * lax.rsqrt(jnp.maximum(_head_sum(kk * kk), 1e-24))
    k2 = k * (1.0 + (a - 1.0) * ka_ref[...])
    bonus = _head_sum(r * k2 * rk_ref[...]) * v

    zv = proj(RWKV_IN + 2 * RET_QK_WIDTH, RET_V_WIDTH)
    zgr = proj(RWKV_IN + 2 * RET_QK_WIDTH + RET_V_WIDTH, RET_V_WIDTH)
    rtg_ref[...] = zgr * jax.nn.sigmoid(zgr)

    gam_l = jnp.concatenate(
        [jnp.broadcast_to(jnp.exp(cs[c * L + L - 1:c * L + L, :]), (L, W)) for c in range(nc)], axis=0)
    ka = kk * a
    gam = jnp.exp(cs)
    inv = jnp.exp(-cs)
    end = gam_l * inv
    row_w = lax.broadcasted_iota(jnp.int32, (tm, W), 0)
    gam_prev = jnp.where((row_w & (L - 1)) == 0, 1.0, pltpu.roll(gam, 1, axis=0))
    rw16_ref[:, 0 * W:1 * W] = (kk * gam_prev).astype(BF16)
    rw16_ref[:, 1 * W:2 * W] = (k2 * inv).astype(BF16)
    rw16_ref[:, 2 * W:3 * W] = (ka * inv).astype(BF16)
    rw16_ref[:, 3 * W:4 * W] = (k2 * end).astype(BF16)
    rw16_ref[:, 4 * W:5 * W] = (ka * end).astype(BF16)
    rw16_ref[:, 5 * W:6 * W] = v.astype(BF16)
    rw32_ref[:, 0 * W:1 * W] = r * gam
    rw32_ref[:, 1 * W:2 * W] = bonus
    rw32_ref[:, 2 * W:3 * W] = g
    rw32_ref[:, 3 * W:4 * W] = gam_l

    half_rows = tm // 2
    half_lanes = RET_QK_DIM // 2
    left = lax.broadcasted_iota(jnp.int32, (half_rows, LANES), 1) < half_lanes
    ang = pos_ref[...] * freq_ref[...]
    ang = jnp.where(left, ang[:half_rows], ang[half_rows:])
    cos_p = jnp.cos(ang)
    sin_p = jnp.sin(ang)
    cos_r = pltpu.roll(cos_p, half_lanes, axis=1)
    sin_r = pltpu.roll(sin_p, half_lanes, axis=1)
    cos2 = jnp.concatenate([jnp.where(left, cos_p, cos_r), jnp.where(left, cos_r, cos_p)], axis=0)
    sin2 = jnp.concatenate([jnp.where(left, -sin_p, sin_r), jnp.where(left, -sin_r, sin_p)], axis=0)

    def rot(x):
        return x * cos2 + pltpu.roll(x, RET_QK_DIM // 2, axis=1) * sin2

    for h in range(RET_HEADS):
        cols = slice(h * RET_QK_DIM, (h + 1) * RET_QK_DIM)
        q = rot(zqk[:, cols])
        kr = rot(zqk[:, RET_QK_WIDTH + h * RET_QK_DIM:RET_QK_WIDTH + (h + 1) * RET_QK_DIM])
        kr = kr * (RET_QK_DIM ** -0.5)
        for part, val in enumerate((q, kr, kr * zeta_ref[h], q * xi_ref[h])):
            c0 = part * RET_QK_WIDTH + h * RET_QK_DIM
            rt16_ref[:, c0:c0 + RET_QK_DIM] = val.astype(BF16)
    rt16_ref[:, 4 * RET_QK_WIDTH:] = zv.astype(BF16)


def _mix_in(h, w, mu, wwa, w0, a0, g_up, k_k, k_a, r_k, pos, freq2, xi_t, zeta_t, *, seq, tm=MIX_BLOCK):
    n = h.shape[0]
    W = RWKV_WIDTH
    t = jnp.arange(tm)
    tril = ((t[:, None] // RWKV_CHUNK == t[None, :] // RWKV_CHUNK) & (t[None, :] <= t[:, None])).astype(BF16)
    tok = lambda width: pl.BlockSpec((tm, width), lambda i: (i, 0))
    vec = _const_spec((1, W))
    widths = (RW16_COLS, RW32_COLS, RT16_COLS, RET_V_WIDTH, GATE_IN)
    dtypes = (BF16, F32, BF16, F32, F32)
    return pl.pallas_call(
        functools.partial(_mixin_kernel, tm=tm, tiles_per_seq=seq // tm),
        grid=(n // tm,),
        in_specs=[
            tok(D_MODEL),
            _const_spec((D_MODEL, RWKV_IN + RET_IN + GATE_IN)),
            _const_spec((tm, tm)),
            _const_spec((1, RWKV_IN)),
            _const_spec((LANES, 2 * W)),
            vec, vec,
            _const_spec((GATE_LORA, W)),
            vec, vec, vec,
            tok(1),
            _const_spec((1, LANES)),
            _const_spec((RET_HEADS, tm, RET_QK_DIM)),
            _const_spec((RET_HEADS, tm, RET_QK_DIM)),
        ],
        out_specs=[tok(width) for width in widths],
        out_shape=[jax.ShapeDtypeStruct((n, width), dt) for width, dt in zip(widths, dtypes)],
        scratch_shapes=[pltpu.VMEM((1, RWKV_IN), F32)],
        compiler_params=pltpu.CompilerParams(
            dimension_semantics=("arbitrary",), vmem_limit_bytes=VMEM_LIMIT),
        name="mix_in",
    )(h, w, tril, mu, wwa, w0, a0, g_up, k_k, k_a, r_k, pos, freq2, xi_t, zeta_t)


def _rwkv_batched(rw16_ref, rw32_ref, chunks, out):
    L = RWKV_CHUNK
    W = RWKV_WIDTH

    def tiles(ref, col0, rows=L):
        return jnp.stack([ref[0, c * L:c * L + rows, col0 + p * LANES:col0 + (p + 1) * LANES]
                          for c in chunks for p in range(RWKV_PAIRS)])

    aq = tiles(rw16_ref, 0 * W)
    kd = tiles(rw16_ref, 1 * W)
    bd = tiles(rw16_ref, 2 * W)
    ke = tiles(rw16_ref, 3 * W)
    be = tiles(rw16_ref, 4 * W)
    vv = tiles(rw16_ref, 5 * W)
    rq32 = tiles(rw32_ref, 0 * W)
    rq = rq32.astype(BF16)
    gam_l = tiles(rw32_ref, 3 * W, rows=1)

    same_head, diag = _same_head_mask(LANES)
    trow = lax.broadcasted_iota(jnp.int32, (L, LANES), 0)
    scol = lax.broadcasted_iota(jnp.int32, (L, LANES), 1)
    m_left = scol < RWKV_HEAD_DIM
    scol = scol & (L - 1)
    strict = scol < trow
    incl = scol <= trow
    eye_pair = jnp.where(scol == trow, 1.0, 0.0).astype(F32)

    def bd2(x):
        zero = jnp.zeros_like(x)
        return jnp.concatenate([jnp.where(m_left, x, zero), jnp.where(m_left, zero, x)], axis=1)

    m = _bdot_nt(jnp.concatenate([aq, rq], axis=1),
                 jnp.concatenate([bd2(kd), bd2(bd)], axis=1))
    mak = jnp.where(strict, m[:, :L, :LANES], 0.0).astype(BF16)
    mab = jnp.where(strict, m[:, :L, LANES:], 0.0)
    mrk = jnp.where(incl, m[:, L:, :LANES], 0.0).astype(BF16)
    mrb = jnp.where(incl, m[:, L:, LANES:], 0.0).astype(BF16)
    yield

    npow = mab.astype(BF16)
    pinv = eye_pair - mab
    for _ in range(5):
        npow = _bdot(npow, bd2(npow)).astype(BF16)
        pinv = pinv + _bdot(pinv, bd2(npow))
        yield

    makv = _bdot(mak, bd2(vv)).astype(BF16)
    au = _bdot(pinv, jnp.concatenate([bd2(aq), bd2(makv)], axis=2)).astype(BF16)
    ahat = au[:, :, :LANES]
    uhat = au[:, :, LANES:]
    yield
    out["qhat"] = rq32 - _bdot(mrb, bd2(ahat))
    out["yhat"] = _bdot(jnp.concatenate([mrk, -mrb], axis=2),
                        jnp.concatenate([bd2(vv), bd2(uhat)], axis=1))
    yield
    bta = _bdot_tn(be, au)
    ktv = _bdot_tn(ke, vv)
    out["gmat"] = jnp.where(same_head, jnp.where(diag, gam_l, 0.0) - bta[:, :, :LANES], 0.0)
    out["hmat"] = jnp.where(same_head, ktv - bta[:, :, LANES:], 0.0)
    yield


def _rwkv_chain(state, res, y_rows):
    for c in range(res["qhat"].shape[0] // RWKV_PAIRS):
        sel = slice(c * RWKV_PAIRS, (c + 1) * RWKV_PAIRS)
        yc = _bdot(res["qhat"][sel], state[0]) + res["yhat"][sel]
        y_rows.append(jnp.concatenate([yc[p] for p in range(RWKV_PAIRS)], axis=1))
        state[0] = _bdot(res["gmat"][sel], state[0]) + res["hmat"][sel]
        yield


def _rwkv_finish(y_rows, rw32_ref, gng_ref, gnb_ref, o_ref):
    W = RWKV_WIDTH
    y = jnp.concatenate(y_rows, axis=0)
    mean = _head_sum(y) * (1.0 / RWKV_HEAD_DIM)
    yc = y - mean
    var = _head_sum(yc * yc) * (1.0 / RWKV_HEAD_DIM)
    yn = yc * lax.rsqrt(var + RWKV_GN_EPS) * gng_ref[...] + gnb_ref[...]
    o_ref[0] = (yn + rw32_ref[0, :, 1 * W:2 * W]) * rw32_ref[0, :, 2 * W:3 * W]


def _ret_head(h, rt16_ref, rtg_ref, mask_ref, cd_ref, o_ref, r_ref, tb):
    c = RET_CHUNK
    nj = tb // c
    rows = [slice(j * c, (j + 1) * c) for j in range(nj)]

    def part(p, r):
        c0 = p * RET_QK_WIDTH + h * RET_QK_DIM
        return rt16_ref[0, r, c0:c0 + RET_QK_DIM]

    v0 = 4 * RET_QK_WIDTH + h * RET_V_DIM
    vb = [rt16_ref[0, r, v0:v0 + RET_V_DIM] for r in rows]
    scores = [_dot_nt(part(0, r), part(1, r)) for r in rows]
    ktv = [_dot_tn(part(2, rows[j]), vb[j]) for j in range(nj)]
    yield
    states = [r_ref[h]]
    for j in range(nj):
        states.append(states[j] * cd_ref[h] + ktv[j])
    r_ref[h] = states[nj]
    lhs = [jnp.concatenate([(scores[j] * mask_ref[h]).astype(BF16), part(3, rows[j])], axis=1)
           for j in range(nj)]
    rhs = [jnp.concatenate([vb[j], states[j].astype(BF16)], axis=0) for j in range(nj)]
    yield
    ys = [jnp.dot(lhs[j], rhs[j], preferred_element_type=F32) for j in range(nj)]
    yield
    for j in range(nj):
        y = ys[j]
        cols = slice(h * RET_V_DIM, (h + 1) * RET_V_DIM)
        g = rtg_ref[0, rows[j], cols]
        mu = jnp.mean(y, -1, keepdims=True)
        yc = y - mu
        var = jnp.mean(yc * yc, -1, keepdims=True)
        o_ref[0, rows[j], cols] = g * (yc * lax.rsqrt(var + LN_EPS))
    yield


def _ret_body(rt16_ref, rtg_ref, mask_ref, cd_ref, o_ref, r_ref, *, tb):
    live = []
    pending = [_ret_head(h, rt16_ref, rtg_ref, mask_ref, cd_ref, o_ref, r_ref, tb) for h in range(RET_HEADS)]
    while live or pending:
        if pending:
            live.append(pending.pop(0))
        for gen in list(live):
            if next(gen, StopIteration) is StopIteration:
                live.remove(gen)
        yield


def _mixers_kernel(rw16_ref, rw32_ref, rt16_ref, rtg_ref, gng_ref, gnb_ref, mask_ref, cd_ref,
                   yr_ref, yt_ref, t_ref, r_ref, *, tb):
    @pl.when(pl.program_id(1) == 0)
    def _():
        t_ref[...] = jnp.zeros_like(t_ref)
        r_ref[...] = jnp.zeros_like(r_ref)

    nc = tb // RWKV_CHUNK
    first, second = {}, {}
    for _ in _rwkv_batched(rw16_ref, rw32_ref, range(nc // 2), first):
        pass
    state = [t_ref[...]]
    y_rows = []
    for _ in itertools.zip_longest(_rwkv_chain(state, first, y_rows),
                                   _rwkv_batched(rw16_ref, rw32_ref, range(nc // 2, nc), second)):
        pass
    for _ in itertools.zip_longest(_rwkv_chain(state, second, y_rows),
                                   _ret_body(rt16_ref, rtg_ref, mask_ref, cd_ref, yt_ref, r_ref, tb=tb)):
        pass
    t_ref[...] = state[0]
    _rwkv_finish(y_rows, rw32_ref, gng_ref, gnb_ref, yr_ref)


def _mixers(rw16, rw32, rt16, rtg, gn_g, gn_b, mask, cd, *, tb=512):
    b, s, _ = rw16.shape
    c = RET_CHUNK
    tok = lambda w: pl.BlockSpec((1, tb, w), lambda i, j: (i, j, 0))
    vec = _const_spec((1, RWKV_WIDTH))
    return pl.pallas_call(
        functools.partial(_mixers_kernel, tb=tb),
        grid=(b, s // tb),
        in_specs=[
            tok(RW16_COLS), tok(RW32_COLS), tok(RT16_COLS), tok(RET_V_WIDTH),
            vec, vec,
            _const_spec((RET_HEADS, c, c)),
            pl.BlockSpec(memory_space=pltpu.SMEM),
        ],
        out_specs=[tok(RWKV_WIDTH), tok(RET_V_WIDTH)],
        out_shape=[jax.ShapeDtypeStruct((b, s, RWKV_WIDTH), F32),
                   jax.ShapeDtypeStruct((b, s, RET_V_WIDTH), F32)],
        scratch_shapes=[
            pltpu.VMEM((RWKV_PAIRS, LANES, LANES), F32),
            pltpu.VMEM((RET_HEADS, RET_QK_DIM, RET_V_DIM), F32),
        ],
        compiler_params=pltpu.CompilerParams(
            dimension_semantics=("arbitrary", "arbitrary"), vmem_limit_bytes=VMEM_LIMIT),
        name="mixers",
    )(rw16, rw32, rt16, rtg, gn_g, gn_b, mask, cd)


def _post_kernel(yr_ref, yt_ref, zg_ref, h_ref, p_ref, wbr_ref, wbt_ref, wo_ref, g2_ref, b2_ref,
                 win_ref, wout_ref, g3_ref, b3_ref, wg_ref, wp_ref, o_ref, acc_ref, *, fc):
    a = _dot(yr_ref[...], wbr_ref[...])
    b = _dot(yt_ref[...], wbt_ref[...])
    m = jax.nn.sigmoid(zg_ref[:, :D_MODEL]) * a + jax.nn.sigmoid(zg_ref[:, D_MODEL:]) * b
    h = _layer_norm(DN_ALPHA * h_ref[...] + _dot(m, wo_ref[...]), g2_ref[...], b2_ref[...])
    h = _ffn_block(h, win_ref, wout_ref, g3_ref, b3_ref, acc_ref, fc)
    gate = jax.nn.sigmoid(_dot(h, wg_ref[...]))
    o_ref[...] = h + gate * _dot(p_ref[...], wp_ref[...])


def _post(yr, yt, zg, h, p, wbr, wbt, wo, g2, b2, w_in, w_out, g3, b3, wg, wp, *, tm=512, fc=256):
    n = h.shape[0]
    tok = lambda w: pl.BlockSpec((tm, w), lambda i: (i, 0))
    vec = _const_spec((1, D_MODEL))
    return pl.pallas_call(
        functools.partial(_post_kernel, fc=fc),
        grid=(n // tm,),
        in_specs=[
            tok(RWKV_WIDTH), tok(RET_V_WIDTH), tok(GATE_IN), tok(D_MODEL), tok(PLE_DIM),
            _const_spec((RWKV_WIDTH, D_MODEL)),
            _const_spec((RET_V_WIDTH, D_MODEL)),
            _const_spec((D_MODEL, D_MODEL)),
            vec, vec,
            _const_spec((D_MODEL, 2 * D_FF)),
            _const_spec((D_FF, D_MODEL)),
            vec, vec,
            _const_spec((D_MODEL, D_MODEL)),
            _const_spec((PLE_DIM, D_MODEL)),
        ],
        out_specs=tok(D_MODEL),
        out_shape=jax.ShapeDtypeStruct((n, D_MODEL), F32),
        scratch_shapes=[pltpu.VMEM((tm, D_MODEL), F32)],
        compiler_params=pltpu.CompilerParams(
            dimension_semantics=("parallel",), vmem_limit_bytes=VMEM_LIMIT),
        name="post",
    )(yr, yt, zg, h, p, wbr, wbt, wo, g2, b2, w_in, w_out, g3, b3, wg, wp)


def _retention_constants(tm):
    h, c = RET_HEADS, RET_CHUNK
    log_gamma = jnp.log(1.0 - jnp.exp2(-5.0 - jnp.arange(h, dtype=F32)))
    idx = jnp.arange(c, dtype=F32)
    rel = idx[:, None] - idx[None, :]
    mask = jnp.where(rel >= 0, jnp.exp(log_gamma[:, None, None] * jnp.maximum(rel, 0.0)), 0.0)
    xi = jnp.exp(log_gamma[:, None] * (idx + 1.0))
    zeta = jnp.exp(log_gamma[:, None] * (c - 1.0 - idx))
    cd = jnp.exp(log_gamma * c)
    xi_t = jnp.broadcast_to(jnp.tile(xi, (1, tm // c))[:, :, None], (h, tm, RET_QK_DIM))
    zeta_t = jnp.broadcast_to(jnp.tile(zeta, (1, tm // c))[:, :, None], (h, tm, RET_QK_DIM))
    half = RET_QK_DIM // 2
    inv_freq = ROPE_BASE ** (-jnp.arange(half, dtype=F32) / half)
    freq2 = jnp.concatenate([inv_freq, inv_freq])[None, :]
    return freq2, mask, xi_t, zeta_t, cd


def kernel(x, p, positions, ln1_g, ln1_b, ffn1_w_in, ffn1_w_out, w_mix_in, rwkv_mu, rwkv_w0, rwkv_w_up,
           rwkv_a0, rwkv_a_up, rwkv_g_up, rwkv_k_k, rwkv_k_a, rwkv_r_k, rwkv_gn_g, rwkv_gn_b,
           w_branch_rwkv, w_branch_ret, w_mix_out, ln2_g, ln2_b, ffn2_w_in, ffn2_w_out, ln3_g, ln3_b,
           ple_w_proj, ple_w_gate):
    b, s, d = x.shape
    n = b * s
    bf = lambda w: w.astype(BF16)
    freq2, mask, xi_t, zeta_t, cd = _retention_constants(MIX_BLOCK)
    pos = positions.astype(F32).reshape(n, 1)
    h = x.reshape(n, d)
    for i in range(DEPTH):
        h = _ffn(h, bf(ffn1_w_in[i]), bf(ffn1_w_out[i]), ln1_g[i][None], ln1_b[i][None])
        zeros = jnp.zeros((DECAY_LORA, RWKV_WIDTH), F32)
        wwa = jnp.concatenate([
            jnp.concatenate([rwkv_w_up[i], zeros], axis=1),
            jnp.concatenate([zeros, rwkv_a_up[i]], axis=1)], axis=0)
        rw16, rw32, rt16, rtg, zg = _mix_in(
            h, bf(w_mix_in[i]), rwkv_mu[i][None], bf(wwa), rwkv_w0[i][None], rwkv_a0[i][None],
            bf(rwkv_g_up[i]), rwkv_k_k[i][None], rwkv_k_a[i][None], rwkv_r_k[i].reshape(1, RWKV_WIDTH),
            pos, freq2, xi_t, zeta_t, seq=s)
        yr, yt = _mixers(rw16.reshape(b, s, RW16_COLS), rw32.reshape(b, s, RW32_COLS),
                         rt16.reshape(b, s, RT16_COLS), rtg.reshape(b, s, RET_V_WIDTH),
                         rwkv_gn_g[i][None], rwkv_gn_b[i][None], mask, cd)
        h = _post(yr.reshape(n, RWKV_WIDTH), yt.reshape(n, RET_V_WIDTH), zg, h, p[i].reshape(n, PLE_DIM),
                  bf(w_branch_rwkv[i]), bf(w_branch_ret[i]), bf(w_mix_out[i]),
                  ln2_g[i][None], ln2_b[i][None],
                  bf(ffn2_w_in[i]), bf(ffn2_w_out[i]), ln3_g[i][None], ln3_b[i][None],
                  bf(ple_w_gate[i]), bf(ple_w_proj[i]))
    return h.reshape(b, s, d)
```

```python
import functools
import itertools
import math

import jax
import jax.numpy as jnp
from jax import lax
from jax.experimental import pallas as pl
from jax.experimental.pallas import tpu as pltpu

F32 = jnp.float32
BF16 = jnp.bfloat16

D_MODEL = 1024
PLE_DIM = 256
D_FF = 2816
RWKV_HEADS = 8
RWKV_HEAD_DIM = 64
RWKV_WIDTH = RWKV_HEADS * RWKV_HEAD_DIM
DECAY_LORA = 64
AAA_LORA = 64
GATE_LORA = 128
RWKV_GN_EPS = 64e-5
DECAY_SCALE = math.exp(-0.5)
RET_HEADS = 4
RET_QK_DIM = 128
RET_V_DIM = 256
RET_QK_WIDTH = RET_HEADS * RET_QK_DIM
RET_V_WIDTH = RET_HEADS * RET_V_DIM
RET_CHUNK = 128
ROPE_BASE = 10000.0
RWKV_IN = 3 * RWKV_WIDTH + DECAY_LORA + AAA_LORA + GATE_LORA
RET_IN = 2 * RET_QK_WIDTH + 2 * RET_V_WIDTH
GATE_IN = 2 * D_MODEL
DEPTH = 1
DN_ALPHA = (2 * DEPTH) ** 0.25
LN_EPS = 1e-5

LANES = 128
RWKV_CHUNK = 64
RWKV_PAIRS = RWKV_WIDTH // LANES
MXU_WIDTH = 256
RWKV_GROUP = MXU_WIDTH
RWKV_GROUPS = RWKV_WIDTH // RWKV_GROUP
VMEM_LIMIT = 56 * 1024 * 1024
MIX_BLOCK = 256

RW16_COLS = 6 * RWKV_WIDTH
RW32_COLS = 4 * RWKV_WIDTH
RT16_COLS = 4 * RET_QK_WIDTH + RET_V_WIDTH


def _dot(a, b):
    return jnp.dot(a.astype(BF16), b.astype(BF16), preferred_element_type=F32)


def _dot_nt(a, b):
    return lax.dot_general(a.astype(BF16), b.astype(BF16), (((1,), (1,)), ((), ())),
                           preferred_element_type=F32)


def _dot_tn(a, b):
    return lax.dot_general(a.astype(BF16), b.astype(BF16), (((0,), (0,)), ((), ())),
                           preferred_element_type=F32)


def _bdot(a, b):
    return lax.dot_general(a.astype(BF16), b.astype(BF16), (((2,), (1,)), ((0,), (0,))),
                           preferred_element_type=F32)


def _bdot_nt(a, b):
    return lax.dot_general(a.astype(BF16), b.astype(BF16), (((2,), (2,)), ((0,), (0,))),
                           preferred_element_type=F32)


def _bdot_tn(a, b):
    return lax.dot_general(a.astype(BF16), b.astype(BF16), (((1,), (1,)), ((0,), (0,))),
                           preferred_element_type=F32)


def _dot_01(x, m01, *, left=False):
    hi = x.astype(BF16)
    lo = (x - hi.astype(F32)).astype(BF16)
    if left:
        dot = lambda t: jnp.dot(m01, t, preferred_element_type=F32)
    else:
        dot = lambda t: jnp.dot(t, m01, preferred_element_type=F32)
    return dot(hi) + dot(lo)


HEAD_SHIFT = RWKV_HEAD_DIM.bit_length() - 1


def _same_head_mask(width):
    jr = lax.broadcasted_iota(jnp.int32, (width, width), 0)
    jc = lax.broadcasted_iota(jnp.int32, (width, width), 1)
    return (jr >> HEAD_SHIFT) == (jc >> HEAD_SHIFT), jr == jc


def _head_sum(x):
    same_head, _ = _same_head_mask(RWKV_GROUP)
    ones_bd = jnp.where(same_head, 1.0, 0.0).astype(BF16)
    return jnp.concatenate(
        [jnp.dot(x[:, q * RWKV_GROUP:(q + 1) * RWKV_GROUP].astype(BF16), ones_bd, preferred_element_type=F32)
         for q in range(RWKV_GROUPS)], axis=1)


def _layer_norm(y, g, b):
    mu = jnp.mean(y, -1, keepdims=True)
    yc = y - mu
    var = jnp.mean(yc * yc, -1, keepdims=True)
    return yc * lax.rsqrt(var + LN_EPS) * g + b


def _const_spec(shape):
    return pl.BlockSpec(shape, lambda *_: (0,) * len(shape), pipeline_mode=pl.Buffered(1))


def _ffn_prenorm(x, win_ref, wout_ref, acc_ref, fc):
    xb = x.astype(BF16)
    for j in range(D_FF // fc):
        gate = jnp.dot(xb, win_ref[:, j * fc:(j + 1) * fc], preferred_element_type=F32)
        up = jnp.dot(xb, win_ref[:, D_FF + j * fc:D_FF + (j + 1) * fc], preferred_element_type=F32)
        act = (gate * jax.nn.sigmoid(gate) * up).astype(BF16)
        part = jnp.dot(act, wout_ref[j * fc:(j + 1) * fc, :], preferred_element_type=F32)
        if j == 0:
            acc_ref[...] = part
        else:
            acc_ref[...] += part
    return DN_ALPHA * x + 0.5 * acc_ref[...]


def _ffn_kernel(x_ref, win_ref, wout_ref, g_ref, b_ref, o_ref, acc_ref, *, fc):
    y = _ffn_prenorm(x_ref[...], win_ref, wout_ref, acc_ref, fc)
    o_ref[...] = _layer_norm(y, g_ref[...], b_ref[...])


def _ffn(x, w_in, w_out, g, b, *, tm=512, fc=256):
    n = x.shape[0]
    return pl.pallas_call(
        functools.partial(_ffn_kernel, fc=fc),
        grid=(n // tm,),
        in_specs=[
            pl.BlockSpec((tm, D_MODEL), lambda i: (i, 0)),
            _const_spec((D_MODEL, 2 * D_FF)),
            _const_spec((D_FF, D_MODEL)),
            _const_spec((1, D_MODEL)),
            _const_spec((1, D_MODEL)),
        ],
        out_specs=pl.BlockSpec((tm, D_MODEL), lambda i: (i, 0)),
        out_shape=jax.ShapeDtypeStruct((n, D_MODEL), F32),
        scratch_shapes=[pltpu.VMEM((tm, D_MODEL), F32)],
        compiler_params=pltpu.CompilerParams(
            dimension_semantics=("parallel",), vmem_limit_bytes=VMEM_LIMIT),
        name="ffn",
    )(x, w_in, w_out, g, b)


def _mixin_kernel(h_ref, w_ref, tril_ref, mu_ref, wwa_ref, w0_ref, a0_ref, gup_ref, kk_ref, ka_ref, rk_ref,
                  pos_ref, freq_ref, xi_ref, zeta_ref,
                  rw16_ref, rw32_ref, rt16_ref, rtg_ref, zg_ref, carry_ref, *, tm, tiles_per_seq):
    L = RWKV_CHUNK
    W = RWKV_WIDTH
    nc = tm // L

    @pl.when(pl.program_id(0) % tiles_per_seq == 0)
    def _():
        carry_ref[...] = jnp.zeros_like(carry_ref)

    hb = h_ref[...].astype(BF16)

    def proj(lo, width):
        return jnp.dot(hb, w_ref[:, lo:lo + width], preferred_element_type=F32)

    z = proj(0, RWKV_IN)
    row = lax.broadcasted_iota(jnp.int32, z.shape, 0)
    prev = jnp.where(row == 0, carry_ref[...], pltpu.roll(z, 1, axis=0))
    carry_ref[...] = z[tm - 1:tm, :]
    zs = z + (prev - z) * mu_ref[...]

    r = zs[:, :W]
    k = zs[:, W:2 * W]
    v = zs[:, 2 * W:3 * W]
    dwa = zs[:, 3 * W:3 * W + LANES]
    dg = zs[:, 3 * W + LANES:]
    lane = lax.broadcasted_iota(jnp.int32, (tm, LANES), 1)
    dwa = jnp.where(lane < RWKV_HEAD_DIM, jnp.tanh(dwa), dwa)

    zqk = proj(RWKV_IN, 2 * RET_QK_WIDTH)

    wa = _dot(dwa, wwa_ref[...])
    g = _dot(jax.nn.sigmoid(dg), gup_ref[...])
    lw = -DECAY_SCALE * jax.nn.sigmoid(w0_ref[...] + wa[:, :W])
    a = jax.nn.sigmoid(a0_ref[...] + wa[:, W:])

    zg_ref[:, :D_MODEL] = proj(RWKV_IN + RET_IN, D_MODEL)

    cs = _dot_01(lw, tril_ref[...], left=True)

    zg_ref[:, D_MODEL:] = proj(RWKV_IN + RET_IN + D_MODEL, D_MODEL)

    kk = k * kk_ref[...]
    kk = kk * lax.rsqrt(jnp.maximum(_head_sum(kk * kk), 1e-24))
    k2 = k * (1.0 + (a - 1.0) * ka_ref[...])
    bonus = _head_sum(r * k2 * rk_ref[...]) * v

    zv = proj(RWKV_IN + 2 * RET_QK_WIDTH, RET_V_WIDTH)
    zgr = proj(RWKV_IN + 2 * RET_QK_WIDTH + RET_V_WIDTH, RET_V_WIDTH)
    rtg_ref[...] = zgr * jax.nn.sigmoid(zgr)

    gam_l = jnp.concatenate(
        [jnp.broadcast_to(jnp.exp(cs[c * L + L - 1:c * L + L, :]), (L, W)) for c in range(nc)], axis=0)
    ka = kk * a
    gam = jnp.exp(cs)
    inv = jnp.exp(-cs)
    end = gam_l * inv
    row_w = lax.broadcasted_iota(jnp.int32, (tm, W), 0)
    gam_prev = jnp.where((row_w & (L - 1)) == 0, 1.0, pltpu.roll(gam, 1, axis=0))
    rw16_ref[:, 0 * W:1 * W] = (kk * gam_prev).astype(BF16)
    rw16_ref[:, 1 * W:2 * W] = (k2 * inv).astype(BF16)
    rw16_ref[:, 2 * W:3 * W] = (ka * inv).astype(BF16)
    rw16_ref[:, 3 * W:4 * W] = (k2 * end).astype(BF16)
    rw16_ref[:, 4 * W:5 * W] = (ka * end).astype(BF16)
    rw16_ref[:, 5 * W:6 * W] = v.astype(BF16)
    rw32_ref[:, 0 * W:1 * W] = r * gam
    rw32_ref[:, 1 * W:2 * W] = bonus
    rw32_ref[:, 2 * W:3 * W] = g
    rw32_ref[:, 3 * W:4 * W] = gam_l

    half_rows = tm // 2
    half_lanes = RET_QK_DIM // 2
    left = lax.broadcasted_iota(jnp.int32, (half_rows, LANES), 1) < half_lanes
    ang = pos_ref[...] * freq_ref[...]
    ang = jnp.where(left, ang[:half_rows], ang[half_rows:])
    cos_p = jnp.cos(ang)
    sin_p = jnp.sin(ang)
    cos_r = pltpu.roll(cos_p, half_lanes, axis=1)
    sin_r = pltpu.roll(sin_p, half_lanes, axis=1)
    cos2 = jnp.concatenate([jnp.where(left, cos_p, cos_r), jnp.where(left, cos_r, cos_p)], axis=0)
    sin2 = jnp.concatenate([jnp.where(left, -sin_p, sin_r), jnp.where(left, -sin_r, sin_p)], axis=0)

    def rot(x):
        return x * cos2 + pltpu.roll(x, RET_QK_DIM // 2, axis=1) * sin2

    for h in range(RET_HEADS):
        cols = slice(h * RET_QK_DIM, (h + 1) * RET_QK_DIM)
        q = rot(zqk[:, cols])
        kr = rot(zqk[:, RET_QK_WIDTH + h * RET_QK_DIM:RET_QK_WIDTH + (h + 1) * RET_QK_DIM])
        kr = kr * (RET_QK_DIM ** -0.5)
        for part, val in enumerate((q, kr, kr * zeta_ref[h], q * xi_ref[h])):
            c0 = part * RET_QK_WIDTH + h * RET_QK_DIM
            rt16_ref[:, c0:c0 + RET_QK_DIM] = val.astype(BF16)
    rt16_ref[:, 4 * RET_QK_WIDTH:] = zv.astype(BF16)


def _mix_in(h, w, mu, wwa, w0, a0, g_up, k_k, k_a, r_k, pos, freq2, xi_t, zeta_t, *, seq, tm=MIX_BLOCK):
    n = h.shape[0]
    W = RWKV_WIDTH
    t = jnp.arange(tm)
    tril = ((t[:, None] // RWKV_CHUNK == t[None, :] // RWKV_CHUNK) & (t[None, :] <= t[:, None])).astype(BF16)
    tok = lambda width: pl.BlockSpec((tm, width), lambda i: (i, 0))
    vec = _const_spec((1, W))
    widths = (RW16_COLS, RW32_COLS, RT16_COLS, RET_V_WIDTH, GATE_IN)
    dtypes = (BF16, F32, BF16, F32, F32)
    return pl.pallas_call(
        functools.partial(_mixin_kernel, tm=tm, tiles_per_seq=seq // tm),
        grid=(n // tm,),
        in_specs=[
            tok(D_MODEL),
            _const_spec((D_MODEL, RWKV_IN + RET_IN + GATE_IN)),
            _const_spec((tm, tm)),
            _const_spec((1, RWKV_IN)),
            _const_spec((LANES, 2 * W)),
            vec, vec,
            _const_spec((GATE_LORA, W)),
            vec, vec, vec,
            tok(1),
            _const_spec((1, LANES)),
            _const_spec((RET_HEADS, tm, RET_QK_DIM)),
            _const_spec((RET_HEADS, tm, RET_QK_DIM)),
        ],
        out_specs=[tok(width) for width in widths],
        out_shape=[jax.ShapeDtypeStruct((n, width), dt) for width, dt in zip(widths, dtypes)],
        scratch_shapes=[pltpu.VMEM((1, RWKV_IN), F32)],
        compiler_params=pltpu.CompilerParams(
            dimension_semantics=("arbitrary",), vmem_limit_bytes=VMEM_LIMIT),
        name="mix_in",
    )(h, w, tril, mu, wwa, w0, a0, g_up, k_k, k_a, r_k, pos, freq2, xi_t, zeta_t)


def _rwkv_batched(rw16_ref, rw32_ref, chunks, out):
    L = RWKV_CHUNK
    W = RWKV_WIDTH

    def tiles(ref, col0, rows=L):
        return jnp.stack([ref[0, c * L:c * L + rows, col0 + p * LANES:col0 + (p + 1) * LANES]
                          for c in chunks for p in range(RWKV_PAIRS)])

    aq = tiles(rw16_ref, 0 * W)
    kd = tiles(rw16_ref, 1 * W)
    bd = tiles(rw16_ref, 2 * W)
    ke = tiles(rw16_ref, 3 * W)
    be = tiles(rw16_ref, 4 * W)
    vv = tiles(rw16_ref, 5 * W)
    rq32 = tiles(rw32_ref, 0 * W)
    rq = rq32.astype(BF16)
    gam_l = tiles(rw32_ref, 3 * W, rows=1)

    same_head, diag = _same_head_mask(LANES)
    trow = lax.broadcasted_iota(jnp.int32, (L, LANES), 0)
    scol = lax.broadcasted_iota(jnp.int32, (L, LANES), 1)
    m_left = scol < RWKV_HEAD_DIM
    scol = scol & (L - 1)
    strict = scol < trow
    incl = scol <= trow
    eye_pair = jnp.where(scol == trow, 1.0, 0.0).astype(F32)

    def bd2(x):
        zero = jnp.zeros_like(x)
        return jnp.concatenate([jnp.where(m_left, x, zero), jnp.where(m_left, zero, x)], axis=1)

    m = _bdot_nt(jnp.concatenate([aq, rq], axis=1),
                 jnp.concatenate([bd2(kd), bd2(bd)], axis=1))
    mak = jnp.where(strict, m[:, :L, :LANES], 0.0).astype(BF16)
    mab = jnp.where(strict, m[:, :L, LANES:], 0.0)
    mrk = jnp.where(incl, m[:, L:, :LANES], 0.0).astype(BF16)
    mrb = jnp.where(incl, m[:, L:, LANES:], 0.0).astype(BF16)
    yield

    npow = mab.astype(BF16)
    pinv = eye_pair - mab
    for _ in range(5):
        npow = _bdot(npow, bd2(npow)).astype(BF16)
        pinv = pinv + _bdot(pinv, bd2(npow))
        yield

    makv = _bdot(mak, bd2(vv)).astype(BF16)
    au = _bdot(pinv, jnp.concatenate([bd2(aq), bd2(makv)], axis=2)).astype(BF16)
    ahat = au[:, :, :LANES]
    uhat = au[:, :, LANES:]
    out.update(
        ahat=ahat, uhat=uhat, rq=rq, vv=vv, vbd=bd2(vv), bd2=bd2, same_head=same_head,
        mrkb=jnp.concatenate([mrk, -mrb], axis=2),
        kbt=jnp.swapaxes(jnp.concatenate([ke, -be], axis=1), 1, 2),
        gcol=jnp.sum(jnp.where(diag, gam_l, 0.0), axis=2, keepdims=True))
    yield


def _rwkv_chain(state, res, y_rows):
    bd2 = res["bd2"]
    for c in range(res["rq"].shape[0] // RWKV_PAIRS):
        sel = slice(c * RWKV_PAIRS, (c + 1) * RWKV_PAIRS)
        t0 = state[0]
        u = (_bdot(res["ahat"][sel], t0) + res["uhat"][sel]).astype(BF16)
        yc = _bdot(res["rq"][sel], t0) + _bdot(
            res["mrkb"][sel], jnp.concatenate([res["vbd"][sel], bd2(u)], axis=1))
        y_rows.append(jnp.concatenate([yc[p] for p in range(RWKV_PAIRS)], axis=1))
        upd = _bdot(res["kbt"][sel], jnp.concatenate([res["vv"][sel], u], axis=1))
        state[0] = res["gcol"][sel] * t0 + jnp.where(res["same_head"], upd, 0.0)
        yield


def _rwkv_finish(y_rows, rw32_ref, gng_ref, gnb_ref, o_ref):
    W = RWKV_WIDTH
    y = jnp.concatenate(y_rows, axis=0)
    mean = _head_sum(y) * (1.0 / RWKV_HEAD_DIM)
    yc = y - mean
    var = _head_sum(yc * yc) * (1.0 / RWKV_HEAD_DIM)
    yn = yc * lax.rsqrt(var + RWKV_GN_EPS) * gng_ref[...] + gnb_ref[...]
    o_ref[0] = (yn + rw32_ref[0, :, 1 * W:2 * W]) * rw32_ref[0, :, 2 * W:3 * W]


def _ret_head(h, rt16_ref, rtg_ref, mask_ref, cd_ref, o_ref, r_ref, tb):
    c = RET_CHUNK
    nj = tb // c
    rows = [slice(j * c, (j + 1) * c) for j in range(nj)]

    def part(p, r):
        c0 = p * RET_QK_WIDTH + h * RET_QK_DIM
        return rt16_ref[0, r, c0:c0 + RET_QK_DIM]

    v0 = 4 * RET_QK_WIDTH + h * RET_V_DIM
    vb = [rt16_ref[0, r, v0:v0 + RET_V_DIM] for r in rows]
    scores = [_dot_nt(part(0, r), part(1, r)) for r in rows]
    ktv = [_dot_tn(part(2, rows[j]), vb[j]) for j in range(nj)]
    yield
    states = [r_ref[h]]
    for j in range(nj):
        states.append(states[j] * cd_ref[h] + ktv[j])
    r_ref[h] = states[nj]
    lhs = [jnp.concatenate([(scores[j] * mask_ref[h]).astype(BF16), part(3, rows[j])], axis=1)
           for j in range(nj)]
    rhs = [jnp.concatenate([vb[j], states[j].astype(BF16)], axis=0) for j in range(nj)]
    yield
    ys = [jnp.dot(lhs[j], rhs[j], preferred_element_type=F32) for j in range(nj)]
    yield
    for j in range(nj):
        y = ys[j]
        cols = slice(h * RET_V_DIM, (h + 1) * RET_V_DIM)
        g = rtg_ref[0, rows[j], cols]
        mu = jnp.mean(y, -1, keepdims=True)
        yc = y - mu
        var = jnp.mean(yc * yc, -1, keepdims=True)
        o_ref[0, rows[j], cols] = g * (yc * lax.rsqrt(var + LN_EPS))
    yield


def _ret_body(rt16_ref, rtg_ref, mask_ref, cd_ref, o_ref, r_ref, *, tb):
    live = []
    pending = [_ret_head(h, rt16_ref, rtg_ref, mask_ref, cd_ref, o_ref, r_ref, tb) for h in range(RET_HEADS)]
    while live or pending:
        if pending:
            live.append(pending.pop(0))
        for gen in list(live):
            if next(gen, StopIteration) is StopIteration:
                live.remove(gen)
        yield


def _mixers_kernel(rw16_ref, rw32_ref, rt16_ref, rtg_ref, gng_ref, gnb_ref, mask_ref, cd_ref,
                   yr_ref, yt_ref, t_ref, r_ref, *, tb):
    @pl.when(pl.program_id(1) == 0)
    def _():
        t_ref[...] = jnp.zeros_like(t_ref)
        r_ref[...] = jnp.zeros_like(r_ref)

    nc = tb // RWKV_CHUNK
    first, second = {}, {}
    for _ in _rwkv_batched(rw16_ref, rw32_ref, range(nc // 2), first):
        pass
    state = [t_ref[...]]
    y_rows = []
    for _ in itertools.zip_longest(_rwkv_chain(state, first, y_rows),
                                   _rwkv_batched(rw16_ref, rw32_ref, range(nc // 2, nc), second)):
        pass
    for _ in itertools.zip_longest(_rwkv_chain(state, second, y_rows),
                                   _ret_body(rt16_ref, rtg_ref, mask_ref, cd_ref, yt_ref, r_ref, tb=tb)):
        pass
    t_ref[...] = state[0]
    _rwkv_finish(y_rows, rw32_ref, gng_ref, gnb_ref, yr_ref)


def _mixers(rw16, rw32, rt16, rtg, gn_g, gn_b, mask, cd, *, tb=512):
    b, s, _ = rw16.shape
    c = RET_CHUNK
    tok = lambda w: pl.BlockSpec((1, tb, w), lambda i, j: (i, j, 0))
    vec = _const_spec((1, RWKV_WIDTH))
    return pl.pallas_call(
        functools.partial(_mixers_kernel, tb=tb),
        grid=(b, s // tb),
        in_specs=[
            tok(RW16_COLS), tok(RW32_COLS), tok(RT16_COLS), tok(RET_V_WIDTH),
            vec, vec,
            _const_spec((RET_HEADS, c, c)),
            pl.BlockSpec(memory_space=pltpu.SMEM),
        ],
        out_specs=[tok(RWKV_WIDTH), tok(RET_V_WIDTH)],
        out_shape=[jax.ShapeDtypeStruct((b, s, RWKV_WIDTH), F32),
                   jax.ShapeDtypeStruct((b, s, RET_V_WIDTH), F32)],
        scratch_shapes=[
            pltpu.VMEM((RWKV_PAIRS, LANES, LANES), F32),
            pltpu.VMEM((RET_HEADS, RET_QK_DIM, RET_V_DIM), F32),
        ],
        compiler_params=pltpu.CompilerParams(
            dimension_semantics=("arbitrary", "arbitrary"), vmem_limit_bytes=VMEM_LIMIT),
        name="mixers",
    )(rw16, rw32, rt16, rtg, gn_g, gn_b, mask, cd)


def _post_kernel(yr_ref, yt_ref, zg_ref, h_ref, p_ref, wbr_ref, wbt_ref, wo_ref, g2_ref, b2_ref,
                 win_ref, wout_ref, g3_ref, b3_ref, wg_ref, wp_ref, o_ref, acc_ref, *, fc):
    a = _dot(yr_ref[...], wbr_ref[...])
    b = _dot(yt_ref[...], wbt_ref[...])
    m = jax.nn.sigmoid(zg_ref[:, :D_MODEL]) * a + jax.nn.sigmoid(zg_ref[:, D_MODEL:]) * b
    half = m.shape[0] // 2
    mix = [_dot(m[:half], wo_ref[...]), _dot(m[half:], wo_ref[...])]
    emb = _dot(p_ref[...], wp_ref[...])
    h = jnp.concatenate(
        [_layer_norm(DN_ALPHA * h_ref[:half, :] + mix[0], g2_ref[...], b2_ref[...]),
         _layer_norm(DN_ALPHA * h_ref[half:, :] + mix[1], g2_ref[...], b2_ref[...])], axis=0)
    y = _ffn_prenorm(h, win_ref, wout_ref, acc_ref, fc)
    for rows in (slice(0, half), slice(half, 2 * half)):
        h3 = _layer_norm(y[rows], g3_ref[...], b3_ref[...])
        gate = jax.nn.sigmoid(_dot(h3, wg_ref[...]))
        o_ref[rows, :] = h3 + gate * emb[rows]


def _post(yr, yt, zg, h, p, wbr, wbt, wo, g2, b2, w_in, w_out, g3, b3, wg, wp, *, tm=512, fc=256):
    n = h.shape[0]
    tok = lambda w: pl.BlockSpec((tm, w), lambda i: (i, 0))
    vec = _const_spec((1, D_MODEL))
    return pl.pallas_call(
        functools.partial(_post_kernel, fc=fc),
        grid=(n // tm,),
        in_specs=[
            tok(RWKV_WIDTH), tok(RET_V_WIDTH), tok(GATE_IN), tok(D_MODEL), tok(PLE_DIM),
            _const_spec((RWKV_WIDTH, D_MODEL)),
            _const_spec((RET_V_WIDTH, D_MODEL)),
            _const_spec((D_MODEL, D_MODEL)),
            vec, vec,
            _const_spec((D_MODEL, 2 * D_FF)),
            _const_spec((D_FF, D_MODEL)),
            vec, vec,
            _const_spec((D_MODEL, D_MODEL)),
            _const_spec((PLE_DIM, D_MODEL)),
        ],
        out_specs=tok(D_MODEL),
        out_shape=jax.ShapeDtypeStruct((n, D_MODEL), F32),
        scratch_shapes=[pltpu.VMEM((tm, D_MODEL), F32)],
        compiler_params=pltpu.CompilerParams(
            dimension_semantics=("parallel",), vmem_limit_bytes=VMEM_LIMIT),
        name="post",
    )(yr, yt, zg, h, p, wbr, wbt, wo, g2, b2, w_in, w_out, g3, b3, wg, wp)


def _retention_constants(tm):
    h, c = RET_HEADS, RET_CHUNK
    log_gamma = jnp.log(1.0 - jnp.exp2(-5.0 - jnp.arange(h, dtype=F32)))
    idx = jnp.arange(c, dtype=F32)
    rel = idx[:, None] - idx[None, :]
    mask = jnp.where(rel >= 0, jnp.exp(log_gamma[:, None, None] * jnp.maximum(rel, 0.0)), 0.0)
    xi = jnp.exp(log_gamma[:, None] * (idx + 1.0))
    zeta = jnp.exp(log_gamma[:, None] * (c - 1.0 - idx))
    cd = jnp.exp(log_gamma * c)
    xi_t = jnp.broadcast_to(jnp.tile(xi, (1, tm // c))[:, :, None], (h, tm, RET_QK_DIM))
    zeta_t = jnp.broadcast_to(jnp.tile(zeta, (1, tm // c))[:, :, None], (h, tm, RET_QK_DIM))
    half = RET_QK_DIM // 2
    inv_freq = ROPE_BASE ** (-jnp.arange(half, dtype=F32) / half)
    freq2 = jnp.concatenate([inv_freq, inv_freq])[None, :]
    return freq2, mask, xi_t, zeta_t, cd


def kernel(x, p, positions, ln1_g, ln1_b, ffn1_w_in, ffn1_w_out, w_mix_in, rwkv_mu, rwkv_w0, rwkv_w_up,
           rwkv_a0, rwkv_a_up, rwkv_g_up, rwkv_k_k, rwkv_k_a, rwkv_r_k, rwkv_gn_g, rwkv_gn_b,
           w_branch_rwkv, w_branch_ret, w_mix_out, ln2_g, ln2_b, ffn2_w_in, ffn2_w_out, ln3_g, ln3_b,
           ple_w_proj, ple_w_gate):
    b, s, d = x.shape
    n = b * s
    bf = lambda w: w.astype(BF16)
    freq2, mask, xi_t, zeta_t, cd = _retention_constants(MIX_BLOCK)
    pos = positions.astype(F32).reshape(n, 1)
    h = x.reshape(n, d)
    for i in range(DEPTH):
        h = _ffn(h, bf(ffn1_w_in[i]), bf(ffn1_w_out[i]), ln1_g[i][None], ln1_b[i][None])
        zeros = jnp.zeros((DECAY_LORA, RWKV_WIDTH), F32)
        wwa = jnp.concatenate([
            jnp.concatenate([rwkv_w_up[i], zeros], axis=1),
            jnp.concatenate([zeros, rwkv_a_up[i]], axis=1)], axis=0)
        rw16, rw32, rt16, rtg, zg = _mix_in(
            h, bf(w_mix_in[i]), rwkv_mu[i][None], bf(wwa), rwkv_w0[i][None], rwkv_a0[i][None],
            bf(rwkv_g_up[i]), rwkv_k_k[i][None], rwkv_k_a[i][None], rwkv_r_k[i].reshape(1, RWKV_WIDTH),
            pos, freq2, xi_t, zeta_t, seq=s)
        yr, yt = _mixers(rw16.reshape(b, s, RW16_COLS), rw32.reshape(b, s, RW32_COLS),
                         rt16.reshape(b, s, RT16_COLS), rtg.reshape(b, s, RET_V_WIDTH),
                         rwkv_gn_g[i][None], rwkv_gn_b[i][None], mask, cd)
        h = _post(yr.reshape(n, RWKV_WIDTH), yt.reshape(n, RET_V_WIDTH), zg, h, p[i].reshape(n, PLE_DIM),
                  bf(w_branch_rwkv[i]), bf(w_branch_ret[i]), bf(w_mix_out[i]),
                  ln2_g[i][None], ln2_b[i][None],
                  bf(ffn2_w_in[i]), bf(ffn2_w_out[i]), ln3_g[i][None], ln3_b[i][None],
                  bf(ple_w_gate[i]), bf(ple_w_proj[i]))
    return h.reshape(b, s, d)
```

```python
import functools
import itertools
import math

import jax
import jax.numpy as jnp
from jax import lax
from jax.experimental import pallas as pl
from jax.experimental.pallas import tpu as pltpu

F32 = jnp.float32
BF16 = jnp.bfloat16

D_MODEL = 1024
PLE_DIM = 256
D_FF = 2816
RWKV_HEADS = 8
RWKV_HEAD_DIM = 64
RWKV_WIDTH = RWKV_HEADS * RWKV_HEAD_DIM
DECAY_LORA = 64
AAA_LORA = 64
GATE_LORA = 128
RWKV_GN_EPS = 64e-5
DECAY_SCALE = math.exp(-0.5)
RET_HEADS = 4
RET_QK_DIM = 128
RET_V_DIM = 256
RET_QK_WIDTH = RET_HEADS * RET_QK_DIM
RET_V_WIDTH = RET_HEADS * RET_V_DIM
RET_CHUNK = 128
ROPE_BASE = 10000.0
RWKV_IN = 3 * RWKV_WIDTH + DECAY_LORA + AAA_LORA + GATE_LORA
RET_IN = 2 * RET_QK_WIDTH + 2 * RET_V_WIDTH
GATE_IN = 2 * D_MODEL
DEPTH = 1
DN_ALPHA = (2 * DEPTH) ** 0.25
LN_EPS = 1e-5

LANES = 128
RWKV_CHUNK = 64
RWKV_PAIRS = RWKV_WIDTH // LANES
MXU_WIDTH = 256
RWKV_GROUP = MXU_WIDTH
RWKV_GROUPS = RWKV_WIDTH // RWKV_GROUP
VMEM_LIMIT = 56 * 1024 * 1024
MIX_BLOCK = 256

RW16_COLS = 6 * RWKV_WIDTH
RW32_COLS = 4 * RWKV_WIDTH
RT16_COLS = 4 * RET_QK_WIDTH + RET_V_WIDTH


def _dot(a, b):
    return jnp.dot(a.astype(BF16), b.astype(BF16), preferred_element_type=F32)


def _dot_nt(a, b):
    return lax.dot_general(a.astype(BF16), b.astype(BF16), (((1,), (1,)), ((), ())),
                           preferred_element_type=F32)


def _dot_tn(a, b):
    return lax.dot_general(a.astype(BF16), b.astype(BF16), (((0,), (0,)), ((), ())),
                           preferred_element_type=F32)


def _bdot(a, b):
    return lax.dot_general(a.astype(BF16), b.astype(BF16), (((2,), (1,)), ((0,), (0,))),
                           preferred_element_type=F32)


def _bdot_nt(a, b):
    return lax.dot_general(a.astype(BF16), b.astype(BF16), (((2,), (2,)), ((0,), (0,))),
                           preferred_element_type=F32)


def _bdot_tn(a, b):
    return lax.dot_general(a.astype(BF16), b.astype(BF16), (((1,), (1,)), ((0,), (0,))),
                           preferred_element_type=F32)


def _dot_01(x, m01, *, left=False):
    hi = x.astype(BF16)
    lo = (x - hi.astype(F32)).astype(BF16)
    if left:
        dot = lambda t: jnp.dot(m01, t, preferred_element_type=F32)
    else:
        dot = lambda t: jnp.dot(t, m01, preferred_element_type=F32)
    return dot(hi) + dot(lo)


HEAD_SHIFT = RWKV_HEAD_DIM.bit_length() - 1


def _same_head_mask(width):
    jr = lax.broadcasted_iota(jnp.int32, (width, width), 0)
    jc = lax.broadcasted_iota(jnp.int32, (width, width), 1)
    return (jr >> HEAD_SHIFT) == (jc >> HEAD_SHIFT), jr == jc


def _head_sum(x):
    same_head, _ = _same_head_mask(RWKV_GROUP)
    ones_bd = jnp.where(same_head, 1.0, 0.0).astype(BF16)
    return jnp.concatenate(
        [jnp.dot(x[:, q * RWKV_GROUP:(q + 1) * RWKV_GROUP].astype(BF16), ones_bd, preferred_element_type=F32)
         for q in range(RWKV_GROUPS)], axis=1)


def _layer_norm(y, g, b):
    mu = jnp.mean(y, -1, keepdims=True)
    yc = y - mu
    var = jnp.mean(yc * yc, -1, keepdims=True)
    return yc * lax.rsqrt(var + LN_EPS) * g + b


def _const_spec(shape):
    return pl.BlockSpec(shape, lambda *_: (0,) * len(shape), pipeline_mode=pl.Buffered(1))


def _ffn_prenorm(x, win_ref, wout_ref, acc_ref, fc):
    xb = x.astype(BF16)
    for j in range(D_FF // fc):
        gate = jnp.dot(xb, win_ref[:, j * fc:(j + 1) * fc], preferred_element_type=F32)
        up = jnp.dot(xb, win_ref[:, D_FF + j * fc:D_FF + (j + 1) * fc], preferred_element_type=F32)
        act = (gate * jax.nn.sigmoid(gate) * up).astype(BF16)
        part = jnp.dot(act, wout_ref[j * fc:(j + 1) * fc, :], preferred_element_type=F32)
        if j == 0:
            acc_ref[...] = part
        else:
            acc_ref[...] += part
    return DN_ALPHA * x + 0.5 * acc_ref[...]


def _ffn_kernel(x_ref, win_ref, wout_ref, g_ref, b_ref, o_ref, acc_ref, *, fc):
    y = _ffn_prenorm(x_ref[...], win_ref, wout_ref, acc_ref, fc)
    o_ref[...] = _layer_norm(y, g_ref[...], b_ref[...])


def _ffn(x, w_in, w_out, g, b, *, tm=512, fc=256):
    n = x.shape[0]
    return pl.pallas_call(
        functools.partial(_ffn_kernel, fc=fc),
        grid=(n // tm,),
        in_specs=[
            pl.BlockSpec((tm, D_MODEL), lambda i: (i, 0)),
            _const_spec((D_MODEL, 2 * D_FF)),
            _const_spec((D_FF, D_MODEL)),
            _const_spec((1, D_MODEL)),
            _const_spec((1, D_MODEL)),
        ],
        out_specs=pl.BlockSpec((tm, D_MODEL), lambda i: (i, 0)),
        out_shape=jax.ShapeDtypeStruct((n, D_MODEL), F32),
        scratch_shapes=[pltpu.VMEM((tm, D_MODEL), F32)],
        compiler_params=pltpu.CompilerParams(
            dimension_semantics=("parallel",), vmem_limit_bytes=VMEM_LIMIT),
        name="ffn",
    )(x, w_in, w_out, g, b)


def _mixin_kernel(h_ref, w_ref, tril_ref, mu_ref, wwa_ref, w0_ref, a0_ref, gup_ref, kk_ref, ka_ref, rk_ref,
                  pos_ref, freq_ref, xi_ref, zeta_ref,
                  rw16_ref, rw32_ref, rt16_ref, rtg_ref, zg_ref, carry_ref, *, tm, tiles_per_seq):
    L = RWKV_CHUNK
    W = RWKV_WIDTH
    nc = tm // L

    @pl.when(pl.program_id(0) % tiles_per_seq == 0)
    def _():
        carry_ref[...] = jnp.zeros_like(carry_ref)

    hb = h_ref[...].astype(BF16)

    def proj(lo, width):
        return jnp.dot(hb, w_ref[:, lo:lo + width], preferred_element_type=F32)

    z = proj(0, RWKV_IN)
    row = lax.broadcasted_iota(jnp.int32, z.shape, 0)
    prev = jnp.where(row == 0, carry_ref[...], pltpu.roll(z, 1, axis=0))
    carry_ref[...] = z[tm - 1:tm, :]
    zs = z + (prev - z) * mu_ref[...]

    r = zs[:, :W]
    k = zs[:, W:2 * W]
    v = zs[:, 2 * W:3 * W]
    dwa = zs[:, 3 * W:3 * W + LANES]
    dg = zs[:, 3 * W + LANES:]
    lane = lax.broadcasted_iota(jnp.int32, (tm, LANES), 1)
    dwa = jnp.where(lane < RWKV_HEAD_DIM, jnp.tanh(dwa), dwa)

    zqk = proj(RWKV_IN, 2 * RET_QK_WIDTH)

    wa = _dot(dwa, wwa_ref[...])
    g = _dot(jax.nn.sigmoid(dg), gup_ref[...])
    lw = -DECAY_SCALE * jax.nn.sigmoid(w0_ref[...] + wa[:, :W])
    a = jax.nn.sigmoid(a0_ref[...] + wa[:, W:])

    zg_ref[:, :D_MODEL] = proj(RWKV_IN + RET_IN, D_MODEL)

    cs = _dot_01(lw, tril_ref[...], left=True)

    zg_ref[:, D_MODEL:] = proj(RWKV_IN + RET_IN + D_MODEL, D_MODEL)

    kk = k * kk_ref[...]
    kk = kk * lax.rsqrt(jnp.maximum(_head_sum(kk * kk), 1e-24))
    k2 = k * (1.0 + (a - 1.0) * ka_ref[...])
    bonus = _head_sum(r * k2 * rk_ref[...]) * v

    zv = proj(RWKV_IN + 2 * RET_QK_WIDTH, RET_V_WIDTH)
    zgr = proj(RWKV_IN + 2 * RET_QK_WIDTH + RET_V_WIDTH, RET_V_WIDTH)
    rtg_ref[...] = zgr * jax.nn.sigmoid(zgr)

    gam_l = jnp.concatenate(
        [jnp.broadcast_to(jnp.exp(cs[c * L + L - 1:c * L + L, :]), (L, W)) for c in range(nc)], axis=0)
    ka = kk * a
    gam = jnp.exp(cs)
    inv = jnp.exp(-cs)
    end = gam_l * inv
    row_w = lax.broadcasted_iota(jnp.int32, (tm, W), 0)
    gam_prev = jnp.where((row_w & (L - 1)) == 0, 1.0, pltpu.roll(gam, 1, axis=0))
    rw16_ref[:, 0 * W:1 * W] = (kk * gam_prev).astype(BF16)
    rw16_ref[:, 1 * W:2 * W] = (k2 * inv).astype(BF16)
    rw16_ref[:, 2 * W:3 * W] = (ka * inv).astype(BF16)
    rw16_ref[:, 3 * W:4 * W] = (k2 * end).astype(BF16)
    rw16_ref[:, 4 * W:5 * W] = (ka * end).astype(BF16)
    rw16_ref[:, 5 * W:6 * W] = v.astype(BF16)
    rw32_ref[:, 0 * W:1 * W] = r * gam
    rw32_ref[:, 1 * W:2 * W] = bonus
    rw32_ref[:, 2 * W:3 * W] = g
    rw32_ref[:, 3 * W:4 * W] = gam_l

    half_rows = tm // 2
    half_lanes = RET_QK_DIM // 2
    left = lax.broadcasted_iota(jnp.int32, (half_rows, LANES), 1) < half_lanes
    ang = pos_ref[...] * freq_ref[...]
    ang = jnp.where(left, ang[:half_rows], ang[half_rows:])
    cos_p = jnp.cos(ang)
    sin_p = jnp.sin(ang)
    cos_r = pltpu.roll(cos_p, half_lanes, axis=1)
    sin_r = pltpu.roll(sin_p, half_lanes, axis=1)
    cos2 = jnp.concatenate([jnp.where(left, cos_p, cos_r), jnp.where(left, cos_r, cos_p)], axis=0)
    sin2 = jnp.concatenate([jnp.where(left, -sin_p, sin_r), jnp.where(left, -sin_r, sin_p)], axis=0)

    def rot(x):
        return x * cos2 + pltpu.roll(x, RET_QK_DIM // 2, axis=1) * sin2

    for h in range(RET_HEADS):
        cols = slice(h * RET_QK_DIM, (h + 1) * RET_QK_DIM)
        q = rot(zqk[:, cols])
        kr = rot(zqk[:, RET_QK_WIDTH + h * RET_QK_DIM:RET_QK_WIDTH + (h + 1) * RET_QK_DIM])
        kr = kr * (RET_QK_DIM ** -0.5)
        for part, val in enumerate((q, kr, kr * zeta_ref[h], q * xi_ref[h])):
            c0 = part * RET_QK_WIDTH + h * RET_QK_DIM
            rt16_ref[:, c0:c0 + RET_QK_DIM] = val.astype(BF16)
    rt16_ref[:, 4 * RET_QK_WIDTH:] = zv.astype(BF16)


def _mix_in(h, w, mu, wwa, w0, a0, g_up, k_k, k_a, r_k, pos, freq2, xi_t, zeta_t, *, seq, tm=MIX_BLOCK):
    n = h.shape[0]
    W = RWKV_WIDTH
    t = jnp.arange(tm)
    tril = ((t[:, None] // RWKV_CHUNK == t[None, :] // RWKV_CHUNK) & (t[None, :] <= t[:, None])).astype(BF16)
    tok = lambda width: pl.BlockSpec((tm, width), lambda i: (i, 0))
    vec = _const_spec((1, W))
    widths = (RW16_COLS, RW32_COLS, RT16_COLS, RET_V_WIDTH, GATE_IN)
    dtypes = (BF16, F32, BF16, F32, F32)
    return pl.pallas_call(
        functools.partial(_mixin_kernel, tm=tm, tiles_per_seq=seq // tm),
        grid=(n // tm,),
        in_specs=[
            tok(D_MODEL),
            _const_spec((D_MODEL, RWKV_IN + RET_IN + GATE_IN)),
            _const_spec((tm, tm)),
            _const_spec((1, RWKV_IN)),
            _const_spec((LANES, 2 * W)),
            vec, vec,
            _const_spec((GATE_LORA, W)),
            vec, vec, vec,
            tok(1),
            _const_spec((1, LANES)),
            _const_spec((RET_HEADS, tm, RET_QK_DIM)),
            _const_spec((RET_HEADS, tm, RET_QK_DIM)),
        ],
        out_specs=[tok(width) for width in widths],
        out_shape=[jax.ShapeDtypeStruct((n, width), dt) for width, dt in zip(widths, dtypes)],
        scratch_shapes=[pltpu.VMEM((1, RWKV_IN), F32)],
        compiler_params=pltpu.CompilerParams(
            dimension_semantics=("arbitrary",), vmem_limit_bytes=VMEM_LIMIT),
        name="mix_in",
    )(h, w, tril, mu, wwa, w0, a0, g_up, k_k, k_a, r_k, pos, freq2, xi_t, zeta_t)


def _rwkv_batched(rw16_ref, rw32_ref, chunks, out):
    L = RWKV_CHUNK
    W = RWKV_WIDTH

    def tiles(ref, col0, rows=L):
        return jnp.stack([ref[0, c * L:c * L + rows, col0 + p * LANES:col0 + (p + 1) * LANES]
                          for c in chunks for p in range(RWKV_PAIRS)])

    aq = tiles(rw16_ref, 0 * W)
    kd = tiles(rw16_ref, 1 * W)
    bd = tiles(rw16_ref, 2 * W)
    ke = tiles(rw16_ref, 3 * W)
    be = tiles(rw16_ref, 4 * W)
    vv = tiles(rw16_ref, 5 * W)
    rq32 = tiles(rw32_ref, 0 * W)
    rq = rq32.astype(BF16)
    gam_l = tiles(rw32_ref, 3 * W, rows=1)

    same_head, diag = _same_head_mask(LANES)
    trow = lax.broadcasted_iota(jnp.int32, (L, LANES), 0)
    scol = lax.broadcasted_iota(jnp.int32, (L, LANES), 1)
    m_left = scol < RWKV_HEAD_DIM
    scol = scol & (L - 1)
    strict = scol < trow
    incl = scol <= trow
    eye_pair = jnp.where(scol == trow, 1.0, 0.0).astype(F32)

    def bd2(x):
        zero = jnp.zeros_like(x)
        return jnp.concatenate([jnp.where(m_left, x, zero), jnp.where(m_left, zero, x)], axis=1)

    m = _bdot_nt(jnp.concatenate([aq, rq], axis=1),
                 jnp.concatenate([bd2(kd), bd2(bd)], axis=1))
    mak = jnp.where(strict, m[:, :L, :LANES], 0.0).astype(BF16)
    mab = jnp.where(strict, m[:, :L, LANES:], 0.0)
    mrk = jnp.where(incl, m[:, L:, :LANES], 0.0).astype(BF16)
    mrb = jnp.where(incl, m[:, L:, LANES:], 0.0).astype(BF16)
    yield

    npow = mab.astype(BF16)
    pinv = eye_pair - mab
    for _ in range(5):
        npow = _bdot(npow, bd2(npow)).astype(BF16)
        pinv = pinv + _bdot(pinv, bd2(npow))
        yield

    makv = _bdot(mak, bd2(vv)).astype(BF16)
    au = _bdot(pinv, jnp.concatenate([bd2(aq), bd2(makv)], axis=2)).astype(BF16)
    ahat = au[:, :, :LANES]
    uhat = au[:, :, LANES:]
    yield
    out["qhat"] = rq32 - _bdot(mrb, bd2(ahat))
    out["yhat"] = _bdot(jnp.concatenate([mrk, -mrb], axis=2),
                        jnp.concatenate([bd2(vv), bd2(uhat)], axis=1))
    yield
    bta = _bdot_tn(be, au)
    ktv = _bdot_tn(ke, vv)
    out["gmat"] = jnp.where(same_head, jnp.where(diag, gam_l, 0.0) - bta[:, :, :LANES], 0.0)
    out["hmat"] = jnp.where(same_head, ktv - bta[:, :, LANES:], 0.0)
    yield


def _rwkv_chain(state, res, y_rows):
    for c in range(res["qhat"].shape[0] // RWKV_PAIRS):
        sel = slice(c * RWKV_PAIRS, (c + 1) * RWKV_PAIRS)
        yc = _bdot(res["qhat"][sel], state[0]) + res["yhat"][sel]
        y_rows.append(jnp.concatenate([yc[p] for p in range(RWKV_PAIRS)], axis=1))
        state[0] = _bdot(res["gmat"][sel], state[0]) + res["hmat"][sel]
        yield


def _rwkv_finish(y_rows, rw32_ref, gng_ref, gnb_ref, o_ref):
    W = RWKV_WIDTH
    y = jnp.concatenate(y_rows, axis=0)
    mean = _head_sum(y) * (1.0 / RWKV_HEAD_DIM)
    yc = y - mean
    var = _head_sum(yc * yc) * (1.0 / RWKV_HEAD_DIM)
    yn = yc * lax.rsqrt(var + RWKV_GN_EPS) * gng_ref[...] + gnb_ref[...]
    o_ref[0] = (yn + rw32_ref[0, :, 1 * W:2 * W]) * rw32_ref[0, :, 2 * W:3 * W]


def _ret_head(h, rt16_ref, rtg_ref, mask_ref, cd_ref, o_ref, r_ref, tb):
    c = RET_CHUNK
    nj = tb // c
    rows = [slice(j * c, (j + 1) * c) for j in range(nj)]

    def part(p, r):
        c0 = p * RET_QK_WIDTH + h * RET_QK_DIM
        return rt16_ref[0, r, c0:c0 + RET_QK_DIM]

    v0 = 4 * RET_QK_WIDTH + h * RET_V_DIM
    vb = [rt16_ref[0, r, v0:v0 + RET_V_DIM] for r in rows]
    scores = [_dot_nt(part(0, r), part(1, r)) for r in rows]
    ktv = [_dot_tn(part(2, rows[j]), vb[j]) for j in range(nj)]
    yield
    states = [r_ref[h]]
    for j in range(nj):
        states.append(states[j] * cd_ref[h] + ktv[j])
    r_ref[h] = states[nj]
    lhs = [jnp.concatenate([(scores[j] * mask_ref[h]).astype(BF16), part(3, rows[j])], axis=1)
           for j in range(nj)]
    rhs = [jnp.concatenate([vb[j], states[j].astype(BF16)], axis=0) for j in range(nj)]
    yield
    ys = [jnp.dot(lhs[j], rhs[j], preferred_element_type=F32) for j in range(nj)]
    yield
    for j in range(nj):
        y = ys[j]
        cols = slice(h * RET_V_DIM, (h + 1) * RET_V_DIM)
        g = rtg_ref[0, rows[j], cols]
        mu = jnp.mean(y, -1, keepdims=True)
        yc = y - mu
        var = jnp.mean(yc * yc, -1, keepdims=True)
        o_ref[0, rows[j], cols] = g * (yc * lax.rsqrt(var + LN_EPS))
    yield


def _ret_body(rt16_ref, rtg_ref, mask_ref, cd_ref, o_ref, r_ref, *, tb):
    live = []
    pending = [_ret_head(h, rt16_ref, rtg_ref, mask_ref, cd_ref, o_ref, r_ref, tb) for h in range(RET_HEADS)]
    while live or pending:
        if pending:
            live.append(pending.pop(0))
        for gen in list(live):
            if next(gen, StopIteration) is StopIteration:
                live.remove(gen)
        yield


def _mixers_kernel(rw16_ref, rw32_ref, rt16_ref, rtg_ref, gng_ref, gnb_ref, mask_ref, cd_ref,
                   yr_ref, yt_ref, t_ref, r_ref, *, tb):
    @pl.when(pl.program_id(1) == 0)
    def _():
        t_ref[...] = jnp.zeros_like(t_ref)
        r_ref[...] = jnp.zeros_like(r_ref)

    nc = tb // RWKV_CHUNK
    first, second = {}, {}
    for _ in _rwkv_batched(rw16_ref, rw32_ref, range(nc // 2), first):
        pass
    state = [t_ref[...]]
    y_rows = []
    for _ in itertools.zip_longest(_rwkv_chain(state, first, y_rows),
                                   _rwkv_batched(rw16_ref, rw32_ref, range(nc // 2, nc), second)):
        pass
    for _ in itertools.zip_longest(_rwkv_chain(state, second, y_rows),
                                   _ret_body(rt16_ref, rtg_ref, mask_ref, cd_ref, yt_ref, r_ref, tb=tb)):
        pass
    t_ref[...] = state[0]
    _rwkv_finish(y_rows, rw32_ref, gng_ref, gnb_ref, yr_ref)


def _mixers(rw16, rw32, rt16, rtg, gn_g, gn_b, mask, cd, *, tb=512):
    b, s, _ = rw16.shape
    c = RET_CHUNK
    tok = lambda w: pl.BlockSpec((1, tb, w), lambda i, j: (i, j, 0))
    vec = _const_spec((1, RWKV_WIDTH))
    return pl.pallas_call(
        functools.partial(_mixers_kernel, tb=tb),
        grid=(b, s // tb),
        in_specs=[
            tok(RW16_COLS), tok(RW32_COLS), tok(RT16_COLS), tok(RET_V_WIDTH),
            vec, vec,
            _const_spec((RET_HEADS, c, c)),
            pl.BlockSpec(memory_space=pltpu.SMEM),
        ],
        out_specs=[tok(RWKV_WIDTH), tok(RET_V_WIDTH)],
        out_shape=[jax.ShapeDtypeStruct((b, s, RWKV_WIDTH), F32),
                   jax.ShapeDtypeStruct((b, s, RET_V_WIDTH), F32)],
        scratch_shapes=[
            pltpu.VMEM((RWKV_PAIRS, LANES, LANES), F32),
            pltpu.VMEM((RET_HEADS, RET_QK_DIM, RET_V_DIM), F32),
        ],
        compiler_params=pltpu.CompilerParams(
            dimension_semantics=("arbitrary", "arbitrary"), vmem_limit_bytes=VMEM_LIMIT),
        name="mixers",
    )(rw16, rw32, rt16, rtg, gn_g, gn_b, mask, cd)


def _post_kernel(yr_ref, yt_ref, zg_ref, h_ref, p_ref, wbr_ref, wbt_ref, wo_ref, g2_ref, b2_ref,
                 win_ref, wout_ref, g3_ref, b3_ref, wg_ref, wp_ref, o_ref, acc_ref, *, fc):
    a = _dot(yr_ref[...], wbr_ref[...])
    b = _dot(yt_ref[...], wbt_ref[...])
    m = jax.nn.sigmoid(zg_ref[:, :D_MODEL]) * a + jax.nn.sigmoid(zg_ref[:, D_MODEL:]) * b
    half = m.shape[0] // 2
    mix = [_dot(m[:half], wo_ref[...]), _dot(m[half:], wo_ref[...])]
    emb = _dot(p_ref[...], wp_ref[...])
    h = jnp.concatenate(
        [_layer_norm(DN_ALPHA * h_ref[:half, :] + mix[0], g2_ref[...], b2_ref[...]),
         _layer_norm(DN_ALPHA * h_ref[half:, :] + mix[1], g2_ref[...], b2_ref[...])], axis=0)
    y = _ffn_prenorm(h, win_ref, wout_ref, acc_ref, fc)
    for rows in (slice(0, half), slice(half, 2 * half)):
        h3 = _layer_norm(y[rows], g3_ref[...], b3_ref[...])
        gate = jax.nn.sigmoid(_dot(h3, wg_ref[...]))
        o_ref[rows, :] = h3 + gate * emb[rows]


def _post(yr, yt, zg, h, p, wbr, wbt, wo, g2, b2, w_in, w_out, g3, b3, wg, wp, *, tm=512, fc=256):
    n = h.shape[0]
    tok = lambda w: pl.BlockSpec((tm, w), lambda i: (i, 0))
    vec = _const_spec((1, D_MODEL))
    return pl.pallas_call(
        functools.partial(_post_kernel, fc=fc),
        grid=(n // tm,),
        in_specs=[
            tok(RWKV_WIDTH), tok(RET_V_WIDTH), tok(GATE_IN), tok(D_MODEL), tok(PLE_DIM),
            _const_spec((RWKV_WIDTH, D_MODEL)),
            _const_spec((RET_V_WIDTH, D_MODEL)),
            _const_spec((D_MODEL, D_MODEL)),
            vec, vec,
            _const_spec((D_MODEL, 2 * D_FF)),
            _const_spec((D_FF, D_MODEL)),
            vec, vec,
            _const_spec((D_MODEL, D_MODEL)),
            _const_spec((PLE_DIM, D_MODEL)),
        ],
        out_specs=tok(D_MODEL),
        out_shape=jax.ShapeDtypeStruct((n, D_MODEL), F32),
        scratch_shapes=[pltpu.VMEM((tm, D_MODEL), F32)],
        compiler_params=pltpu.CompilerParams(
            dimension_semantics=("parallel",), vmem_limit_bytes=VMEM_LIMIT),
        name="post",
    )(yr, yt, zg, h, p, wbr, wbt, wo, g2, b2, w_in, w_out, g3, b3, wg, wp)


def _retention_constants(tm):
    h, c = RET_HEADS, RET_CHUNK
    log_gamma = jnp.log(1.0 - jnp.exp2(-5.0 - jnp.arange(h, dtype=F32)))
    idx = jnp.arange(c, dtype=F32)
    rel = idx[:, None] - idx[None, :]
    mask = jnp.where(rel >= 0, jnp.exp(log_gamma[:, None, None] * jnp.maximum(rel, 0.0)), 0.0)
    xi = jnp.exp(log_gamma[:, None] * (idx + 1.0))
    zeta = jnp.exp(log_gamma[:, None] * (c - 1.0 - idx))
    cd = jnp.exp(log_gamma * c)
    xi_t = jnp.broadcast_to(jnp.tile(xi, (1, tm // c))[:, :, None], (h, tm, RET_QK_DIM))
    zeta_t = jnp.broadcast_to(jnp.tile(zeta, (1, tm // c))[:, :, None], (h, tm, RET_QK_DIM))
    half = RET_QK_DIM // 2
    inv_freq = ROPE_BASE ** (-jnp.arange(half, dtype=F32) / half)
    freq2 = jnp.concatenate([inv_freq, inv_freq])[None, :]
    return freq2, mask, xi_t, zeta_t, cd


def kernel(x, p, positions, ln1_g, ln1_b, ffn1_w_in, ffn1_w_out, w_mix_in, rwkv_mu, rwkv_w0, rwkv_w_up,
           rwkv_a0, rwkv_a_up, rwkv_g_up, rwkv_k_k, rwkv_k_a, rwkv_r_k, rwkv_gn_g, rwkv_gn_b,
           w_branch_rwkv, w_branch_ret, w_mix_out, ln2_g, ln2_b, ffn2_w_in, ffn2_w_out, ln3_g, ln3_b,
           ple_w_proj, ple_w_gate):
    b, s, d = x.shape
    n = b * s
    bf = lambda w: w.astype(BF16)
    freq2, mask, xi_t, zeta_t, cd = _retention_constants(MIX_BLOCK)
    pos = positions.astype(F32).reshape(n, 1)
    h = x.reshape(n, d)
    for i in range(DEPTH):
        h = _ffn(h, bf(ffn1_w_in[i]), bf(ffn1_w_out[i]), ln1_g[i][None], ln1_b[i][None])
        zeros = jnp.zeros((DECAY_LORA, RWKV_WIDTH), F32)
        wwa = jnp.concatenate([
            jnp.concatenate([rwkv_w_up[i], zeros], axis=1),
            jnp.concatenate([zeros, rwkv_a_up[i]], axis=1)], axis=0)
        rw16, rw32, rt16, rtg, zg = _mix_in(
            h, bf(w_mix_in[i]), rwkv_mu[i][None], bf(wwa), rwkv_w0[i][None], rwkv_a0[i][None],
            bf(rwkv_g_up[i]), rwkv_k_k[i][None], rwkv_k_a[i][None], rwkv_r_k[i].reshape(1, RWKV_WIDTH),
            pos, freq2, xi_t, zeta_t, seq=s)
        yr, yt = _mixers(rw16.reshape(b, s, RW16_COLS), rw32.reshape(b, s, RW32_COLS),
                         rt16.reshape(b, s, RT16_COLS), rtg.reshape(b, s, RET_V_WIDTH),
                         rwkv_gn_g[i][None], rwkv_gn_b[i][None], mask, cd)
        h = _post(yr.reshape(n, RWKV_WIDTH), yt.reshape(n, RET_V_WIDTH), zg, h, p[i].reshape(n, PLE_DIM),
                  bf(w_branch_rwkv[i]), bf(w_branch_ret[i]), bf(w_mix_out[i]),
                  ln2_g[i][None], ln2_b[i][None],
                  bf(ffn2_w_in[i]), bf(ffn2_w_out[i]), ln3_g[i][None], ln3_b[i][None],
                  bf(ple_w_gate[i]), bf(ple_w_proj[i]))
    return h.reshape(b, s, d)
```

```python
import functools
import itertools
import math

import jax
import jax.numpy as jnp
from jax import lax
from jax.experimental import pallas as pl
from jax.experimental.pallas import tpu as pltpu

F32 = jnp.float32
BF16 = jnp.bfloat16

D_MODEL = 1024
PLE_DIM = 256
D_FF = 2816
RWKV_HEADS = 8
RWKV_HEAD_DIM = 64
RWKV_WIDTH = RWKV_HEADS * RWKV_HEAD_DIM
DECAY_LORA = 64
AAA_LORA = 64
GATE_LORA = 128
RWKV_GN_EPS = 64e-5
DECAY_SCALE = math.exp(-0.5)
RET_HEADS = 4
RET_QK_DIM = 128
RET_V_DIM = 256
RET_QK_WIDTH = RET_HEADS * RET_QK_DIM
RET_V_WIDTH = RET_HEADS * RET_V_DIM
RET_CHUNK = 128
ROPE_BASE = 10000.0
RWKV_IN = 3 * RWKV_WIDTH + DECAY_LORA + AAA_LORA + GATE_LORA
RET_IN = 2 * RET_QK_WIDTH + 2 * RET_V_WIDTH
GATE_IN = 2 * D_MODEL
DEPTH = 1
DN_ALPHA = (2 * DEPTH) ** 0.25
LN_EPS = 1e-5

LANES = 128
RWKV_CHUNK = 64
RWKV_PAIRS = RWKV_WIDTH // LANES
MXU_WIDTH = 256
RWKV_GROUP = MXU_WIDTH
RWKV_GROUPS = RWKV_WIDTH // RWKV_GROUP
VMEM_LIMIT = 56 * 1024 * 1024
MIX_BLOCK = 256

RW16_COLS = 6 * RWKV_WIDTH
RW32_COLS = 4 * RWKV_WIDTH
RT16_COLS = 4 * RET_QK_WIDTH + RET_V_WIDTH


def _dot(a, b):
    return jnp.dot(a.astype(BF16), b.astype(BF16), preferred_element_type=F32)


def _dot_nt(a, b):
    return lax.dot_general(a.astype(BF16), b.astype(BF16), (((1,), (1,)), ((), ())),
                           preferred_element_type=F32)


def _dot_tn(a, b):
    return lax.dot_general(a.astype(BF16), b.astype(BF16), (((0,), (0,)), ((), ())),
                           preferred_element_type=F32)


def _bdot(a, b):
    return lax.dot_general(a.astype(BF16), b.astype(BF16), (((2,), (1,)), ((0,), (0,))),
                           preferred_element_type=F32)


def _bdot_nt(a, b):
    return lax.dot_general(a.astype(BF16), b.astype(BF16), (((2,), (2,)), ((0,), (0,))),
                           preferred_element_type=F32)


def _bdot_tn(a, b):
    return lax.dot_general(a.astype(BF16), b.astype(BF16), (((1,), (1,)), ((0,), (0,))),
                           preferred_element_type=F32)


def _dot_01(x, m01, *, left=False):
    hi = x.astype(BF16)
    lo = (x - hi.astype(F32)).astype(BF16)
    if left:
        dot = lambda t: jnp.dot(m01, t, preferred_element_type=F32)
    else:
        dot = lambda t: jnp.dot(t, m01, preferred_element_type=F32)
    return dot(hi) + dot(lo)


HEAD_SHIFT = RWKV_HEAD_DIM.bit_length() - 1


def _same_head_mask(width):
    jr = lax.broadcasted_iota(jnp.int32, (width, width), 0)
    jc = lax.broadcasted_iota(jnp.int32, (width, width), 1)
    return (jr >> HEAD_SHIFT) == (jc >> HEAD_SHIFT), jr == jc


def _head_sum(x):
    same_head, _ = _same_head_mask(RWKV_GROUP)
    ones_bd = jnp.where(same_head, 1.0, 0.0).astype(BF16)
    return jnp.concatenate(
        [jnp.dot(x[:, q * RWKV_GROUP:(q + 1) * RWKV_GROUP].astype(BF16), ones_bd, preferred_element_type=F32)
         for q in range(RWKV_GROUPS)], axis=1)


def _layer_norm(y, g, b):
    mu = jnp.mean(y, -1, keepdims=True)
    yc = y - mu
    var = jnp.mean(yc * yc, -1, keepdims=True)
    return yc * lax.rsqrt(var + LN_EPS) * g + b


def _const_spec(shape):
    return pl.BlockSpec(shape, lambda *_: (0,) * len(shape), pipeline_mode=pl.Buffered(1))


def _zero_after(v):
    bits = lax.bitcast_convert_type(v, jnp.uint32)
    return lax.bitcast_convert_type((bits >> 16) >> 16, F32)


def _ffn_prenorm(x, win_ref, wout_ref, acc_ref, fc, tie=None):
    xb = x.astype(BF16)
    groups = D_FF // fc
    for j in range(groups):
        gate = jnp.dot(xb, win_ref[:, j * fc:(j + 1) * fc], preferred_element_type=F32)
        up = jnp.dot(xb, win_ref[:, D_FF + j * fc:D_FF + (j + 1) * fc], preferred_element_type=F32)
        act = (gate * jax.nn.sigmoid(gate) * up).astype(BF16)
        part = jnp.dot(act, wout_ref[j * fc:(j + 1) * fc, :], preferred_element_type=F32)
        if tie is not None and j == groups // 2:
            part = part + tie
        if j == 0:
            acc_ref[...] = part
        else:
            acc_ref[...] += part
    return DN_ALPHA * x + 0.5 * acc_ref[...]


def _ffn_kernel(x_ref, win_ref, wout_ref, g_ref, b_ref, o_ref, acc_ref, y_ref, *, fc):
    @pl.when(pl.program_id(0) == 0)
    def _():
        y_ref[...] = jnp.zeros_like(y_ref)

    normed = _layer_norm(y_ref[...], g_ref[...], b_ref[...])
    o_ref[...] = normed
    y_ref[...] = _ffn_prenorm(x_ref[...], win_ref, wout_ref, acc_ref, fc, tie=_zero_after(normed))


def _ffn(x, w_in, w_out, g, b, *, tm=512, fc=256):
    n = x.shape[0]
    tiles = n // tm
    return pl.pallas_call(
        functools.partial(_ffn_kernel, fc=fc),
        grid=(tiles + 1,),
        in_specs=[
            pl.BlockSpec((tm, D_MODEL), lambda i: (jnp.minimum(i, tiles - 1), 0)),
            _const_spec((D_MODEL, 2 * D_FF)),
            _const_spec((D_FF, D_MODEL)),
            _const_spec((1, D_MODEL)),
            _const_spec((1, D_MODEL)),
        ],
        out_specs=pl.BlockSpec((tm, D_MODEL), lambda i: (jnp.maximum(i - 1, 0), 0)),
        out_shape=jax.ShapeDtypeStruct((n, D_MODEL), F32),
        scratch_shapes=[pltpu.VMEM((tm, D_MODEL), F32),
                        pltpu.VMEM((tm, D_MODEL), F32)],
        compiler_params=pltpu.CompilerParams(
            dimension_semantics=("arbitrary",), vmem_limit_bytes=VMEM_LIMIT),
        name="ffn",
    )(x, w_in, w_out, g, b)


def _mixin_kernel(h_ref, w_ref, tril_ref, mu_ref, wwa_ref, w0_ref, a0_ref, gup_ref, kk_ref, ka_ref, rk_ref,
                  pos_ref, freq_ref, xi_ref, zeta_ref,
                  rw16_ref, rw32_ref, rt16_ref, rtg_ref, zg_ref, carry_ref, *, tm, tiles_per_seq):
    L = RWKV_CHUNK
    W = RWKV_WIDTH
    nc = tm // L

    @pl.when(pl.program_id(0) % tiles_per_seq == 0)
    def _():
        carry_ref[...] = jnp.zeros_like(carry_ref)

    hb = h_ref[...].astype(BF16)

    def proj(lo, width):
        return jnp.dot(hb, w_ref[:, lo:lo + width], preferred_element_type=F32)

    z = proj(0, RWKV_IN)
    row = lax.broadcasted_iota(jnp.int32, z.shape, 0)
    prev = jnp.where(row == 0, carry_ref[...], pltpu.roll(z, 1, axis=0))
    carry_ref[...] = z[tm - 1:tm, :]
    zs = z + (prev - z) * mu_ref[...]

    r = zs[:, :W]
    k = zs[:, W:2 * W]
    v = zs[:, 2 * W:3 * W]
    dwa = zs[:, 3 * W:3 * W + LANES]
    dg = zs[:, 3 * W + LANES:]
    lane = lax.broadcasted_iota(jnp.int32, (tm, LANES), 1)
    dwa = jnp.where(lane < RWKV_HEAD_DIM, jnp.tanh(dwa), dwa)

    zqk = proj(RWKV_IN, 2 * RET_QK_WIDTH)

    wa = _dot(dwa, wwa_ref[...])
    g = _dot(jax.nn.sigmoid(dg), gup_ref[...])
    lw = -DECAY_SCALE * jax.nn.sigmoid(w0_ref[...] + wa[:, :W])
    a = jax.nn.sigmoid(a0_ref[...] + wa[:, W:])

    zg_ref[:, :D_MODEL] = proj(RWKV_IN + RET_IN, D_MODEL)

    cs = _dot_01(lw, tril_ref[...], left=True)

    zg_ref[:, D_MODEL:] = proj(RWKV_IN + RET_IN + D_MODEL, D_MODEL)

    kk = k * kk_ref[...]
    kk = kk * lax.rsqrt(jnp.maximum(_head_sum(kk * kk), 1e-24))
    k2 = k * (1.0 + (a - 1.0) * ka_ref[...])
    bonus = _head_sum(r * k2 * rk_ref[...]) * v

    zv = proj(RWKV_IN + 2 * RET_QK_WIDTH, RET_V_WIDTH)
    zgr = proj(RWKV_IN + 2 * RET_QK_WIDTH + RET_V_WIDTH, RET_V_WIDTH)
    rtg_ref[...] = zgr * jax.nn.sigmoid(zgr)

    gam_l = jnp.concatenate(
        [jnp.broadcast_to(jnp.exp(cs[c * L + L - 1:c * L + L, :]), (L, W)) for c in range(nc)], axis=0)
    ka = kk * a
    gam = jnp.exp(cs)
    inv = jnp.exp(-cs)
    end = gam_l * inv
    row_w = lax.broadcasted_iota(jnp.int32, (tm, W), 0)
    gam_prev = jnp.where((row_w & (L - 1)) == 0, 1.0, pltpu.roll(gam, 1, axis=0))
    rw16_ref[:, 0 * W:1 * W] = (kk * gam_prev).astype(BF16)
    rw16_ref[:, 1 * W:2 * W] = (k2 * inv).astype(BF16)
    rw16_ref[:, 2 * W:3 * W] = (ka * inv).astype(BF16)
    rw16_ref[:, 3 * W:4 * W] = (k2 * end).astype(BF16)
    rw16_ref[:, 4 * W:5 * W] = (ka * end).astype(BF16)
    rw16_ref[:, 5 * W:6 * W] = v.astype(BF16)
    rw32_ref[:, 0 * W:1 * W] = r * gam
    rw32_ref[:, 1 * W:2 * W] = bonus
    rw32_ref[:, 2 * W:3 * W] = g
    rw32_ref[:, 3 * W:4 * W] = gam_l

    half_rows = tm // 2
    half_lanes = RET_QK_DIM // 2
    left = lax.broadcasted_iota(jnp.int32, (half_rows, LANES), 1) < half_lanes
    ang = pos_ref[...] * freq_ref[...]
    ang = jnp.where(left, ang[:half_rows], ang[half_rows:])
    cos_p = jnp.cos(ang)
    sin_p = jnp.sin(ang)
    cos_r = pltpu.roll(cos_p, half_lanes, axis=1)
    sin_r = pltpu.roll(sin_p, half_lanes, axis=1)
    cos2 = jnp.concatenate([jnp.where(left, cos_p, cos_r), jnp.where(left, cos_r, cos_p)], axis=0)
    sin2 = jnp.concatenate([jnp.where(left, -sin_p, sin_r), jnp.where(left, -sin_r, sin_p)], axis=0)

    def rot(x):
        return x * cos2 + pltpu.roll(x, RET_QK_DIM // 2, axis=1) * sin2

    for h in range(RET_HEADS):
        cols = slice(h * RET_QK_DIM, (h + 1) * RET_QK_DIM)
        q = rot(zqk[:, cols])
        kr = rot(zqk[:, RET_QK_WIDTH + h * RET_QK_DIM:RET_QK_WIDTH + (h + 1) * RET_QK_DIM])
        kr = kr * (RET_QK_DIM ** -0.5)
        for part, val in enumerate((q, kr, kr * zeta_ref[h], q * xi_ref[h])):
            c0 = part * RET_QK_WIDTH + h * RET_QK_DIM
            rt16_ref[:, c0:c0 + RET_QK_DIM] = val.astype(BF16)
    rt16_ref[:, 4 * RET_QK_WIDTH:] = zv.astype(BF16)


def _mix_in(h, w, mu, wwa, w0, a0, g_up, k_k, k_a, r_k, pos, freq2, xi_t, zeta_t, *, seq, tm=MIX_BLOCK):
    n = h.shape[0]
    W = RWKV_WIDTH
    t = jnp.arange(tm)
    tril = ((t[:, None] // RWKV_CHUNK == t[None, :] // RWKV_CHUNK) & (t[None, :] <= t[:, None])).astype(BF16)
    tok = lambda width: pl.BlockSpec((tm, width), lambda i: (i, 0))
    vec = _const_spec((1, W))
    widths = (RW16_COLS, RW32_COLS, RT16_COLS, RET_V_WIDTH, GATE_IN)
    dtypes = (BF16, F32, BF16, F32, F32)
    return pl.pallas_call(
        functools.partial(_mixin_kernel, tm=tm, tiles_per_seq=seq // tm),
        grid=(n // tm,),
        in_specs=[
            tok(D_MODEL),
            _const_spec((D_MODEL, RWKV_IN + RET_IN + GATE_IN)),
            _const_spec((tm, tm)),
            _const_spec((1, RWKV_IN)),
            _const_spec((LANES, 2 * W)),
            vec, vec,
            _const_spec((GATE_LORA, W)),
            vec, vec, vec,
            tok(1),
            _const_spec((1, LANES)),
            _const_spec((RET_HEADS, tm, RET_QK_DIM)),
            _const_spec((RET_HEADS, tm, RET_QK_DIM)),
        ],
        out_specs=[tok(width) for width in widths],
        out_shape=[jax.ShapeDtypeStruct((n, width), dt) for width, dt in zip(widths, dtypes)],
        scratch_shapes=[pltpu.VMEM((1, RWKV_IN), F32)],
        compiler_params=pltpu.CompilerParams(
            dimension_semantics=("arbitrary",), vmem_limit_bytes=VMEM_LIMIT),
        name="mix_in",
    )(h, w, tril, mu, wwa, w0, a0, g_up, k_k, k_a, r_k, pos, freq2, xi_t, zeta_t)


def _rwkv_batched(rw16_ref, rw32_ref, chunks, out):
    L = RWKV_CHUNK
    W = RWKV_WIDTH

    def tiles(ref, col0, rows=L):
        return jnp.stack([ref[0, c * L:c * L + rows, col0 + p * LANES:col0 + (p + 1) * LANES]
                          for c in chunks for p in range(RWKV_PAIRS)])

    aq = tiles(rw16_ref, 0 * W)
    kd = tiles(rw16_ref, 1 * W)
    bd = tiles(rw16_ref, 2 * W)
    ke = tiles(rw16_ref, 3 * W)
    be = tiles(rw16_ref, 4 * W)
    vv = tiles(rw16_ref, 5 * W)
    rq32 = tiles(rw32_ref, 0 * W)
    rq = rq32.astype(BF16)
    gam_l = tiles(rw32_ref, 3 * W, rows=1)

    same_head, diag = _same_head_mask(LANES)
    trow = lax.broadcasted_iota(jnp.int32, (L, LANES), 0)
    scol = lax.broadcasted_iota(jnp.int32, (L, LANES), 1)
    m_left = scol < RWKV_HEAD_DIM
    scol = scol & (L - 1)
    strict = scol < trow
    incl = scol <= trow
    eye_pair = jnp.where(scol == trow, 1.0, 0.0).astype(F32)

    def bd2(x):
        zero = jnp.zeros_like(x)
        return jnp.concatenate([jnp.where(m_left, x, zero), jnp.where(m_left, zero, x)], axis=1)

    m = _bdot_nt(jnp.concatenate([aq, rq], axis=1),
                 jnp.concatenate([bd2(kd), bd2(bd)], axis=1))
    mak = jnp.where(strict, m[:, :L, :LANES], 0.0).astype(BF16)
    mab = jnp.where(strict, m[:, :L, LANES:], 0.0)
    mrk = jnp.where(incl, m[:, L:, :LANES], 0.0).astype(BF16)
    mrb = jnp.where(incl, m[:, L:, LANES:], 0.0).astype(BF16)
    yield

    npow = mab.astype(BF16)
    pinv = eye_pair - mab
    for _ in range(5):
        npow = _bdot(npow, bd2(npow)).astype(BF16)
        pinv = pinv + _bdot(pinv, bd2(npow))
        yield

    makv = _bdot(mak, bd2(vv)).astype(BF16)
    au = _bdot(pinv, jnp.concatenate([bd2(aq), bd2(makv)], axis=2)).astype(BF16)
    ahat = au[:, :, :LANES]
    uhat = au[:, :, LANES:]
    yield
    out["qhat"] = rq32 - _bdot(mrb, bd2(ahat))
    out["yhat"] = _bdot(jnp.concatenate([mrk, -mrb], axis=2),
                        jnp.concatenate([bd2(vv), bd2(uhat)], axis=1))
    yield
    bta = _bdot_tn(be, au)
    ktv = _bdot_tn(ke, vv)
    out["gmat"] = jnp.where(same_head, jnp.where(diag, gam_l, 0.0) - bta[:, :, :LANES], 0.0)
    out["hmat"] = jnp.where(same_head, ktv - bta[:, :, LANES:], 0.0)
    yield


def _rwkv_chain(state, res, y_rows):
    for c in range(res["qhat"].shape[0] // RWKV_PAIRS):
        sel = slice(c * RWKV_PAIRS, (c + 1) * RWKV_PAIRS)
        yc = _bdot(res["qhat"][sel], state[0]) + res["yhat"][sel]
        y_rows.append(jnp.concatenate([yc[p] for p in range(RWKV_PAIRS)], axis=1))
        state[0] = _bdot(res["gmat"][sel], state[0]) + res["hmat"][sel]
        yield


def _rwkv_finish(y_rows, rw32_ref, gng_ref, gnb_ref, o_ref):
    W = RWKV_WIDTH
    y = jnp.concatenate(y_rows, axis=0)
    mean = _head_sum(y) * (1.0 / RWKV_HEAD_DIM)
    yc = y - mean
    var = _head_sum(yc * yc) * (1.0 / RWKV_HEAD_DIM)
    yn = yc * lax.rsqrt(var + RWKV_GN_EPS) * gng_ref[...] + gnb_ref[...]
    o_ref[0] = (yn + rw32_ref[0, :, 1 * W:2 * W]) * rw32_ref[0, :, 2 * W:3 * W]


def _ret_head(h, rt16_ref, rtg_ref, mask_ref, cd_ref, o_ref, r_ref, tb):
    c = RET_CHUNK
    nj = tb // c
    rows = [slice(j * c, (j + 1) * c) for j in range(nj)]

    def part(p, r):
        c0 = p * RET_QK_WIDTH + h * RET_QK_DIM
        return rt16_ref[0, r, c0:c0 + RET_QK_DIM]

    v0 = 4 * RET_QK_WIDTH + h * RET_V_DIM
    vb = [rt16_ref[0, r, v0:v0 + RET_V_DIM] for r in rows]
    scores = [_dot_nt(part(0, r), part(1, r)) for r in rows]
    ktv = [_dot_tn(part(2, rows[j]), vb[j]) for j in range(nj)]
    yield
    states = [r_ref[h]]
    for j in range(nj):
        states.append(states[j] * cd_ref[h] + ktv[j])
    r_ref[h] = states[nj]
    lhs = [jnp.concatenate([(scores[j] * mask_ref[h]).astype(BF16), part(3, rows[j])], axis=1)
           for j in range(nj)]
    rhs = [jnp.concatenate([vb[j], states[j].astype(BF16)], axis=0) for j in range(nj)]
    yield
    ys = [jnp.dot(lhs[j], rhs[j], preferred_element_type=F32) for j in range(nj)]
    yield
    for j in range(nj):
        y = ys[j]
        cols = slice(h * RET_V_DIM, (h + 1) * RET_V_DIM)
        g = rtg_ref[0, rows[j], cols]
        mu = jnp.mean(y, -1, keepdims=True)
        yc = y - mu
        var = jnp.mean(yc * yc, -1, keepdims=True)
        o_ref[0, rows[j], cols] = g * (yc * lax.rsqrt(var + LN_EPS))
    yield


def _ret_body(rt16_ref, rtg_ref, mask_ref, cd_ref, o_ref, r_ref, *, tb):
    live = []
    pending = [_ret_head(h, rt16_ref, rtg_ref, mask_ref, cd_ref, o_ref, r_ref, tb) for h in range(RET_HEADS)]
    while live or pending:
        if pending:
            live.append(pending.pop(0))
        for gen in list(live):
            if next(gen, StopIteration) is StopIteration:
                live.remove(gen)
        yield


def _mixers_kernel(rw16_ref, rw32_ref, rt16_ref, rtg_ref, gng_ref, gnb_ref, mask_ref, cd_ref,
                   yr_ref, yt_ref, t_ref, r_ref, *, tb):
    @pl.when(pl.program_id(1) == 0)
    def _():
        t_ref[...] = jnp.zeros_like(t_ref)
        r_ref[...] = jnp.zeros_like(r_ref)

    nc = tb // RWKV_CHUNK
    first, second = {}, {}
    for _ in _rwkv_batched(rw16_ref, rw32_ref, range(nc // 2), first):
        pass
    state = [t_ref[...]]
    y_rows = []
    for _ in itertools.zip_longest(_rwkv_chain(state, first, y_rows),
                                   _rwkv_batched(rw16_ref, rw32_ref, range(nc // 2, nc), second)):
        pass
    for _ in itertools.zip_longest(_rwkv_chain(state, second, y_rows),
                                   _ret_body(rt16_ref, rtg_ref, mask_ref, cd_ref, yt_ref, r_ref, tb=tb)):
        pass
    t_ref[...] = state[0]
    _rwkv_finish(y_rows, rw32_ref, gng_ref, gnb_ref, yr_ref)


def _mixers(rw16, rw32, rt16, rtg, gn_g, gn_b, mask, cd, *, tb=512):
    b, s, _ = rw16.shape
    c = RET_CHUNK
    tok = lambda w: pl.BlockSpec((1, tb, w), lambda i, j: (i, j, 0))
    vec = _const_spec((1, RWKV_WIDTH))
    return pl.pallas_call(
        functools.partial(_mixers_kernel, tb=tb),
        grid=(b, s // tb),
        in_specs=[
            tok(RW16_COLS), tok(RW32_COLS), tok(RT16_COLS), tok(RET_V_WIDTH),
            vec, vec,
            _const_spec((RET_HEADS, c, c)),
            pl.BlockSpec(memory_space=pltpu.SMEM),
        ],
        out_specs=[tok(RWKV_WIDTH), tok(RET_V_WIDTH)],
        out_shape=[jax.ShapeDtypeStruct((b, s, RWKV_WIDTH), F32),
                   jax.ShapeDtypeStruct((b, s, RET_V_WIDTH), F32)],
        scratch_shapes=[
            pltpu.VMEM((RWKV_PAIRS, LANES, LANES), F32),
            pltpu.VMEM((RET_HEADS, RET_QK_DIM, RET_V_DIM), F32),
        ],
        compiler_params=pltpu.CompilerParams(
            dimension_semantics=("arbitrary", "arbitrary"), vmem_limit_bytes=VMEM_LIMIT),
        name="mixers",
    )(rw16, rw32, rt16, rtg, gn_g, gn_b, mask, cd)


def _post_kernel(yr_ref, yt_ref, zg_ref, h_ref, p_ref, wbr_ref, wbt_ref, wo_ref, g2_ref, b2_ref,
                 win_ref, wout_ref, g3_ref, b3_ref, wg_ref, wp_ref, o_ref, acc_ref, *, fc):
    a = _dot(yr_ref[...], wbr_ref[...])
    b = _dot(yt_ref[...], wbt_ref[...])
    m = jax.nn.sigmoid(zg_ref[:, :D_MODEL]) * a + jax.nn.sigmoid(zg_ref[:, D_MODEL:]) * b
    half = m.shape[0] // 2
    mix = [_dot(m[:half], wo_ref[...]), _dot(m[half:], wo_ref[...])]
    emb = _dot(p_ref[...], wp_ref[...])
    h = jnp.concatenate(
        [_layer_norm(DN_ALPHA * h_ref[:half, :] + mix[0], g2_ref[...], b2_ref[...]),
         _layer_norm(DN_ALPHA * h_ref[half:, :] + mix[1], g2_ref[...], b2_ref[...])], axis=0)
    y = _ffn_prenorm(h, win_ref, wout_ref, acc_ref, fc)
    for rows in (slice(0, half), slice(half, 2 * half)):
        h3 = _layer_norm(y[rows], g3_ref[...], b3_ref[...])
        gate = jax.nn.sigmoid(_dot(h3, wg_ref[...]))
        o_ref[rows, :] = h3 + gate * emb[rows]


def _post(yr, yt, zg, h, p, wbr, wbt, wo, g2, b2, w_in, w_out, g3, b3, wg, wp, *, tm=512, fc=256):
    n = h.shape[0]
    tok = lambda w: pl.BlockSpec((tm, w), lambda i: (i, 0))
    vec = _const_spec((1, D_MODEL))
    return pl.pallas_call(
        functools.partial(_post_kernel, fc=fc),
        grid=(n // tm,),
        in_specs=[
            tok(RWKV_WIDTH), tok(RET_V_WIDTH), tok(GATE_IN), tok(D_MODEL), tok(PLE_DIM),
            _const_spec((RWKV_WIDTH, D_MODEL)),
            _const_spec((RET_V_WIDTH, D_MODEL)),
            _const_spec((D_MODEL, D_MODEL)),
            vec, vec,
            _const_spec((D_MODEL, 2 * D_FF)),
            _const_spec((D_FF, D_MODEL)),
            vec, vec,
            _const_spec((D_MODEL, D_MODEL)),
            _const_spec((PLE_DIM, D_MODEL)),
        ],
        out_specs=tok(D_MODEL),
        out_shape=jax.ShapeDtypeStruct((n, D_MODEL), F32),
        scratch_shapes=[pltpu.VMEM((tm, D_MODEL), F32)],
        compiler_params=pltpu.CompilerParams(
            dimension_semantics=("parallel",), vmem_limit_bytes=VMEM_LIMIT),
        name="post",
    )(yr, yt, zg, h, p, wbr, wbt, wo, g2, b2, w_in, w_out, g3, b3, wg, wp)


def _retention_constants(tm):
    h, c = RET_HEADS, RET_CHUNK
    log_gamma = jnp.log(1.0 - jnp.exp2(-5.0 - jnp.arange(h, dtype=F32)))
    idx = jnp.arange(c, dtype=F32)
    rel = idx[:, None] - idx[None, :]
    mask = jnp.where(rel >= 0, jnp.exp(log_gamma[:, None, None] * jnp.maximum(rel, 0.0)), 0.0)
    xi = jnp.exp(log_gamma[:, None] * (idx + 1.0))
    zeta = jnp.exp(log_gamma[:, None] * (c - 1.0 - idx))
    cd = jnp.exp(log_gamma * c)
    xi_t = jnp.broadcast_to(jnp.tile(xi, (1, tm // c))[:, :, None], (h, tm, RET_QK_DIM))
    zeta_t = jnp.broadcast_to(jnp.tile(zeta, (1, tm // c))[:, :, None], (h, tm, RET_QK_DIM))
    half = RET_QK_DIM // 2
    inv_freq = ROPE_BASE ** (-jnp.arange(half, dtype=F32) / half)
    freq2 = jnp.concatenate([inv_freq, inv_freq])[None, :]
    return freq2, mask, xi_t, zeta_t, cd


def kernel(x, p, positions, ln1_g, ln1_b, ffn1_w_in, ffn1_w_out, w_mix_in, rwkv_mu, rwkv_w0, rwkv_w_up,
           rwkv_a0, rwkv_a_up, rwkv_g_up, rwkv_k_k, rwkv_k_a, rwkv_r_k, rwkv_gn_g, rwkv_gn_b,
           w_branch_rwkv, w_branch_ret, w_mix_out, ln2_g, ln2_b, ffn2_w_in, ffn2_w_out, ln3_g, ln3_b,
           ple_w_proj, ple_w_gate):
    b, s, d = x.shape
    n = b * s
    bf = lambda w: w.astype(BF16)
    freq2, mask, xi_t, zeta_t, cd = _retention_constants(MIX_BLOCK)
    pos = positions.astype(F32).reshape(n, 1)
    h = x.reshape(n, d)
    for i in range(DEPTH):
        h = _ffn(h, bf(ffn1_w_in[i]), bf(ffn1_w_out[i]), ln1_g[i][None], ln1_b[i][None])
        zeros = jnp.zeros((DECAY_LORA, RWKV_WIDTH), F32)
        wwa = jnp.concatenate([
            jnp.concatenate([rwkv_w_up[i], zeros], axis=1),
            jnp.concatenate([zeros, rwkv_a_up[i]], axis=1)], axis=0)
        rw16, rw32, rt16, rtg, zg = _mix_in(
            h, bf(w_mix_in[i]), rwkv_mu[i][None], bf(wwa), rwkv_w0[i][None], rwkv_a0[i][None],
            bf(rwkv_g_up[i]), rwkv_k_k[i][None], rwkv_k_a[i][None], rwkv_r_k[i].reshape(1, RWKV_WIDTH),
            pos, freq2, xi_t, zeta_t, seq=s)
        yr, yt = _mixers(rw16.reshape(b, s, RW16_COLS), rw32.reshape(b, s, RW32_COLS),
                         rt16.reshape(b, s, RT16_COLS), rtg.reshape(b, s, RET_V_WIDTH),
                         rwkv_gn_g[i][None], rwkv_gn_b[i][None], mask, cd)
        h = _post(yr.reshape(n, RWKV_WIDTH), yt.reshape(n, RET_V_WIDTH), zg, h, p[i].reshape(n, PLE_DIM),
                  bf(w_branch_rwkv[i]), bf(w_branch_ret[i]), bf(w_mix_out[i]),
                  ln2_g[i][None], ln2_b[i][None],
                  bf(ffn2_w_in[i]), bf(ffn2_w_out[i]), ln3_g[i][None], ln3_b[i][None],
                  bf(ple_w_gate[i]), bf(ple_w_proj[i]))
    return h.reshape(b, s, d)
```
